```python
import math
import jax, jax.numpy as jnp
from jax import lax
import numpy as np

D_MODEL = 2048
BATCH = 2
SEQ = 16384
DEPTH = 2

N_EVEN = (DEPTH + 1) // 2
N_ODD = DEPTH // 2

DN_ALPHA = (2.0 * DEPTH) ** 0.25
DN_BETA = (8.0 * DEPTH) ** -0.25
LN_EPS = 1e-5

CS_BLOCK = 128

POOL_WIDTH = D_MODEL // 2
POOL_WINDOWS = (2, 4, 8, 16)
POOL_GROUPS = len(POOL_WINDOWS)
POOL_GROUP_DIM = POOL_WIDTH // POOL_GROUPS

SB_HEADS = 8
SB_HEAD_DIM = (D_MODEL - POOL_WIDTH) // SB_HEADS
SB_WIDTH = SB_HEADS * SB_HEAD_DIM
SB_BLOCK = 128

EVEN_IN = POOL_WIDTH + 3 * SB_WIDTH
EVEN_MIX = POOL_WIDTH + SB_WIDTH

SGU_WIDTH = D_MODEL
SGU_GROUPS = 8
SGU_GROUP_DIM = SGU_WIDTH // SGU_GROUPS
SGU_CHUNK = 128
ODD_IN = 2 * SGU_WIDTH

FFN_DIM = 5632
N_EXPERTS = 8
TOP_K = 2
EXPERT_DIM = 1408

kernel_name = 'hybrid_pool_stickbreak_sgu_moe_deepnorm'


def layer_norm(x, g, b):
    xf = x.astype(jnp.float32)
    mu = jnp.mean(xf, axis=-1, keepdims=True)
    xc = xf - mu
    var = jnp.mean(xc * xc, axis=-1, keepdims=True)
    y = xc * lax.rsqrt(var + LN_EPS)
    return (y * g.astype(jnp.float32) + b.astype(jnp.float32)).astype(x.dtype)


def block_cumsum(x):
    lead = x.shape[:-1]
    L = x.shape[-1]
    nb = L // CS_BLOCK
    xb = x.astype(jnp.float32).reshape(lead + (nb, CS_BLOCK))
    tri = jnp.triu(jnp.ones((CS_BLOCK, CS_BLOCK), jnp.float32))
    within = jnp.matmul(xb, tri, precision=lax.Precision.HIGHEST)
    totals = within[..., -1]
    stri = jnp.triu(jnp.ones((nb, nb), jnp.float32), k=1)
    prefix = jnp.matmul(totals, stri, precision=lax.Precision.HIGHEST)
    return (within + prefix[..., None]).reshape(lead + (L,))


def swiglu(x, w_gu, w_down):
    gate, up = jnp.split(x @ w_gu, 2, axis=-1)
    return (jax.nn.silu(gate) * up) @ w_down


def pool_mixer(p, pool_w, pool_scale):
    B, S, _ = p.shape
    pf = p.astype(jnp.float32)
    cs = jnp.swapaxes(block_cumsum(jnp.swapaxes(pf, 1, 2)), 1, 2)
    pos = jnp.arange(S)
    outs = []
    for g, w in enumerate(POOL_WINDOWS):
        sl = slice(g * POOL_GROUP_DIM, (g + 1) * POOL_GROUP_DIM)
        c = cs[..., sl]
        lagged = jnp.pad(c, ((0, 0), (w, 0), (0, 0)))[:, :S]
        count = jnp.minimum(pos + 1, w).astype(jnp.float32)[None, :, None]
        outs.append((c - lagged) / count - pf[..., sl])
    d = jnp.stack(outs, axis=2).astype(p.dtype)
    y = jnp.einsum('bsgc,gcd->bsgd', d, pool_w).reshape(B, S, POOL_WIDTH)
    return y * pool_scale


def stick_breaking_attention(q, k, v):
    B, S, H, Dh = q.shape
    qh = jnp.transpose(q, (0, 2, 1, 3))
    kh = jnp.transpose(k, (0, 2, 1, 3))
    vh = jnp.transpose(v, (0, 2, 1, 3))
    n_blocks = S // SB_BLOCK
    scale = 1.0 / math.sqrt(Dh)
    outs = []
    for i in range(n_blocks):
        start = i * SB_BLOCK
        kv_len = start + SB_BLOCK
        qb = qh[:, :, start:kv_len]
        kb = kh[:, :, :kv_len]
        vb = vh[:, :, :kv_len]
        z = jnp.einsum('bhqd,bhkd->bhqk', qb, kb,
                       preferred_element_type=jnp.float32) * scale
        q_pos = start + jnp.arange(SB_BLOCK)
        causal = jnp.arange(kv_len)[None, :] < q_pos[:, None]
        log_not = jnp.where(causal, jax.nn.log_sigmoid(-z), 0.0)
        c = block_cumsum(log_not)
        between = c[..., -1:] - c
        a = jnp.where(causal, jnp.exp(jax.nn.log_sigmoid(z) + between), 0.0)
        outs.append(jnp.einsum('bhqk,bhkd->bhqd', a.astype(vb.dtype), vb))
    o = jnp.concatenate(outs, axis=2)
    return jnp.transpose(o, (0, 2, 1, 3)).reshape(B, S, H * Dh)


def spatial_gating(z, ln_g, ln_b, w_s, b_s):
    B, S, _ = z.shape
    z = jax.nn.gelu(z)
    u, v = jnp.split(z, 2, axis=-1)
    v = layer_norm(v, ln_g, ln_b)
    v = v.reshape(B, S // SGU_CHUNK, SGU_CHUNK, SGU_GROUPS, SGU_GROUP_DIM)
    mask = jnp.tril(jnp.ones((SGU_CHUNK, SGU_CHUNK), dtype=bool))
    w = jnp.where(mask[None], w_s, 0.0).astype(v.dtype)
    mixed = jnp.einsum('gts,bnsgc->bntgc', w, v) + jnp.transpose(b_s)[None, None, :, :, None]
    return u * mixed.reshape(B, S, SGU_WIDTH)


def moe_swiglu(x, w_router, w_gu, w_down):
    B, S, Dm = x.shape
    xt = x.reshape(B * S, Dm)
    logits = jnp.dot(xt, w_router, preferred_element_type=jnp.float32)
    top_vals, top_idx = lax.top_k(logits, TOP_K)
    top_w = jax.nn.softmax(top_vals, axis=-1)
    gates = jnp.sum(jax.nn.one_hot(top_idx, N_EXPERTS, dtype=jnp.float32)
                    * top_w[..., None], axis=1)
    y = jnp.zeros(xt.shape, jnp.float32)
    for e in range(N_EXPERTS):
        y = y + gates[:, e:e + 1] * swiglu(xt, w_gu[e], w_down[e])
    return y.astype(x.dtype).reshape(B, S, Dm)


def _normal(key, shape, scale):
    return jax.random.normal(key, shape, jnp.float32) * scale


def setup_inputs(seed: int = 0) -> dict:
    key = jax.random.key(seed)
    ks = jax.random.split(key, 32)
    D = D_MODEL
    return {
        'x': _normal(ks[0], (BATCH, SEQ, D), 1.0),
        'even_w_in': _normal(ks[1], (N_EVEN, D, EVEN_IN), D ** -0.5),
        'even_pool_w': _normal(ks[2], (N_EVEN, POOL_GROUPS, POOL_GROUP_DIM, POOL_GROUP_DIM), POOL_GROUP_DIM ** -0.5),
        'even_pool_scale': 1.0 + _normal(ks[3], (N_EVEN, POOL_WIDTH), 0.02),
        'even_w_out': _normal(ks[4], (N_EVEN, EVEN_MIX, D), DN_BETA * EVEN_MIX ** -0.5),
        'even_ln1_g': 1.0 + _normal(ks[5], (N_EVEN, D), 0.02),
        'even_ln1_b': _normal(ks[6], (N_EVEN, D), 0.02),
        'even_ffn_w_gu': _normal(ks[7], (N_EVEN, D, 2 * FFN_DIM), D ** -0.5),
        'even_ffn_w_down': _normal(ks[8], (N_EVEN, FFN_DIM, D), DN_BETA * FFN_DIM ** -0.5),
        'even_ln2_g': 1.0 + _normal(ks[9], (N_EVEN, D), 0.02),
        'even_ln2_b': _normal(ks[10], (N_EVEN, D), 0.02),
        'odd_w_in': _normal(ks[11], (N_ODD, D, ODD_IN), D ** -0.5),
        'odd_sgu_ln_g': 1.0 + _normal(ks[12], (N_ODD, SGU_WIDTH), 0.02),
        'odd_sgu_ln_b': _normal(ks[13], (N_ODD, SGU_WIDTH), 0.02),
        'odd_sgu_w': _normal(ks[14], (N_ODD, SGU_GROUPS, SGU_CHUNK, SGU_CHUNK), SGU_CHUNK ** -0.5),
        'odd_sgu_b': 1.0 + _normal(ks[15], (N_ODD, SGU_GROUPS, SGU_CHUNK), 0.02),
        'odd_w_out': _normal(ks[16], (N_ODD, SGU_WIDTH, D), DN_BETA * SGU_WIDTH ** -0.5),
        'odd_ln1_g': 1.0 + _normal(ks[17], (N_ODD, D), 0.02),
        'odd_ln1_b': _normal(ks[18], (N_ODD, D), 0.02),
        'odd_router': _normal(ks[19], (N_ODD, D, N_EXPERTS), D ** -0.5),
        'odd_moe_w_gu': _normal(ks[20], (N_ODD, N_EXPERTS, D, 2 * EXPERT_DIM), D ** -0.5),
        'odd_moe_w_down': _normal(ks[21], (N_ODD, N_EXPERTS, EXPERT_DIM, D), DN_BETA * EXPERT_DIM ** -0.5),
        'odd_ln2_g': 1.0 + _normal(ks[22], (N_ODD, D), 0.02),
        'odd_ln2_b': _normal(ks[23], (N_ODD, D), 0.02),
    }


def reference(x, even_w_in, even_pool_w, even_pool_scale, even_w_out, even_ln1_g, even_ln1_b,
              even_ffn_w_gu, even_ffn_w_down, even_ln2_g, even_ln2_b,
              odd_w_in, odd_sgu_ln_g, odd_sgu_ln_b, odd_sgu_w, odd_sgu_b, odd_w_out,
              odd_ln1_g, odd_ln1_b, odd_router, odd_moe_w_gu, odd_moe_w_down,
              odd_ln2_g, odd_ln2_b):
    B, S, _ = x.shape
    h = x
    for layer in range(DEPTH):
        i = layer // 2
        if layer % 2 == 0:
            proj = h @ even_w_in[i]
            p = proj[..., :POOL_WIDTH]
            q, k, v = jnp.split(proj[..., POOL_WIDTH:], 3, axis=-1)
            q = q.reshape(B, S, SB_HEADS, SB_HEAD_DIM)
            k = k.reshape(B, S, SB_HEADS, SB_HEAD_DIM)
            v = v.reshape(B, S, SB_HEADS, SB_HEAD_DIM)
            mixed = jnp.concatenate(
                [pool_mixer(p, even_pool_w[i], even_pool_scale[i]),
                 stick_breaking_attention(q, k, v)], axis=-1)
            h = layer_norm(DN_ALPHA * h + mixed @ even_w_out[i], even_ln1_g[i], even_ln1_b[i])
            h = layer_norm(DN_ALPHA * h + swiglu(h, even_ffn_w_gu[i], even_ffn_w_down[i]),
                           even_ln2_g[i], even_ln2_b[i])
        else:
            z = h @ odd_w_in[i]
            mixed = spatial_gating(z, odd_sgu_ln_g[i], odd_sgu_ln_b[i], odd_sgu_w[i], odd_sgu_b[i])
            h = layer_norm(DN_ALPHA * h + mixed @ odd_w_out[i], odd_ln1_g[i], odd_ln1_b[i])
            h = layer_norm(DN_ALPHA * h + moe_swiglu(h, odd_router[i], odd_moe_w_gu[i], odd_moe_w_down[i]),
                           odd_ln2_g[i], odd_ln2_b[i])
    return h
```

```python
import functools
import math

import jax
import jax.numpy as jnp
from jax import lax
from jax.experimental import pallas as pl
from jax.experimental.pallas import tpu as pltpu

F32 = jnp.float32
BF16 = jnp.bfloat16

LN_EPS = 1e-5
POOL_WINDOWS = (2, 4, 8, 16)
SB_HEAD_DIM = 128
SGU_GROUPS = 8
SGU_CHUNK = 128
TOP_K = 2

V7X_LANES = 128
V7X_VMEM_BYTES = 64 * 1024 * 1024
BLOCK = 128

F32_EXP_UNDERFLOW = -104.0


def _vmem_limit(block_bytes):
    return int(min(block_bytes * 1.3 + (6 << 20), V7X_VMEM_BYTES - (6 << 20)))


def _layer_norm(r, g, b):
    mu = jnp.mean(r, axis=-1, keepdims=True)
    xc = r - mu
    var = jnp.mean(xc * xc, axis=-1, keepdims=True)
    return xc * lax.rsqrt(var + LN_EPS) * g + b


def _silu(x):
    return x * jax.nn.sigmoid(x)


def _gelu_tanh(x):
    c = math.sqrt(2.0 / math.pi)
    return 0.5 * x * (1.0 + jnp.tanh(c * (x + 0.044715 * (x * x * x))))


def _split_bf16(x):
    hi = x.astype(BF16)
    lo = (x - hi.astype(F32)).astype(BF16)
    return hi, lo


def _const_spec(shape):
    nd = len(shape)
    return pl.BlockSpec(shape, lambda *_: (0,) * nd)


def _in_proj_even_kernel(x_ref, w_ref, p_ref, qkv_ref, *, pool_width):
    xb = x_ref[...].astype(BF16)
    p_ref[...] = jnp.dot(xb, w_ref[:, :pool_width], preferred_element_type=F32)
    n_rest = (w_ref.shape[1] - pool_width) // pool_width
    for c in range(n_rest):
        lo = pool_width * (c + 1)
        qkv_ref[:, c * pool_width:(c + 1) * pool_width] = jnp.dot(
            xb, w_ref[:, lo:lo + pool_width], preferred_element_type=F32).astype(BF16)


def _in_proj_even(x2, w_in, pool_width, tm):
    m, d = x2.shape
    n = w_in.shape[1]
    blk = 2 * tm * d * 4 + 2 * d * n * 2 + 2 * tm * pool_width * 4 + 2 * tm * (n - pool_width) * 2 + tm * n * 4
    return pl.pallas_call(
        functools.partial(_in_proj_even_kernel, pool_width=pool_width),
        grid=(m // tm,),
        in_specs=[pl.BlockSpec((tm, d), lambda i: (i, 0)), _const_spec((d, n))],
        out_specs=[pl.BlockSpec((tm, pool_width), lambda i: (i, 0)),
                   pl.BlockSpec((tm, n - pool_width), lambda i: (i, 0))],
        out_shape=[jax.ShapeDtypeStruct((m, pool_width), F32),
                   jax.ShapeDtypeStruct((m, n - pool_width), BF16)],
        compiler_params=pltpu.CompilerParams(
            dimension_semantics=("arbitrary",), vmem_limit_bytes=_vmem_limit(blk)),
        name="in_proj_even",
    )(x2, w_in)


def _pool_kernel(pc_ref, pp_ref, w_ref, sc_ref, o_ref, hi_ref, lo_ref, *, ts, group_dim):
    i = pl.program_id(1)
    prev = jnp.where(i > 0, pp_ref[0], 0.0)
    h, l = _split_bf16(prev)
    hi_ref[0:BLOCK, :] = h
    lo_ref[0:BLOCK, :] = l
    h, l = _split_bf16(pc_ref[0])
    hi_ref[BLOCK:, :] = h
    lo_ref[BLOCK:, :] = l

    ii = lax.broadcasted_iota(jnp.int32, (BLOCK, 2 * BLOCK), 0) + BLOCK
    jj = lax.broadcasted_iota(jnp.int32, (BLOCK, 2 * BLOCK), 1)
    row = lax.broadcasted_iota(jnp.int32, (BLOCK, 1), 0)
    for g, w in enumerate(POOL_WINDOWS):
        band = jnp.where((jj <= ii) & (jj > ii - w), 1.0, 0.0).astype(BF16)
        cs = slice(g * group_dim, (g + 1) * group_dim)
        for r in range(ts // BLOCK):
            rs = slice(r * BLOCK, r * BLOCK + 2 * BLOCK)
            wsum = (jnp.dot(band, hi_ref[rs, cs], preferred_element_type=F32)
                    + jnp.dot(band, lo_ref[rs, cs], preferred_element_type=F32))
            t = i * ts + r * BLOCK + row
            count = jnp.minimum(t + 1, w).astype(F32)
            d = wsum / count - pc_ref[0, r * BLOCK:(r + 1) * BLOCK, cs]
            y = jnp.dot(d.astype(BF16), w_ref[g], preferred_element_type=F32) * sc_ref[:, cs]
            o_ref[0, r * BLOCK:(r + 1) * BLOCK, cs] = y.astype(o_ref.dtype)


def _pool_mixer(p, pool_w, pool_scale, ts):
    b, s, c = p.shape
    groups = len(POOL_WINDOWS)
    gd = c // groups
    per = ts // BLOCK
    blk = 2 * ts * c * 4 + 2 * BLOCK * c * 4 + 2 * groups * gd * gd * 2 + 2 * ts * c * 2 + 2 * (ts + BLOCK) * c * 2
    return pl.pallas_call(
        functools.partial(_pool_kernel, ts=ts, group_dim=gd),
        grid=(b, s // ts),
        in_specs=[pl.BlockSpec((1, ts, c), lambda bi, i: (bi, i, 0)),
                  pl.BlockSpec((1, BLOCK, c), lambda bi, i: (bi, jnp.maximum(i * per - 1, 0), 0)),
                  _const_spec((groups, gd, gd)),
                  _const_spec((1, c))],
        out_specs=pl.BlockSpec((1, ts, c), lambda bi, i: (bi, i, 0)),
        out_shape=jax.ShapeDtypeStruct((b, s, c), BF16),
        scratch_shapes=[pltpu.VMEM((ts + BLOCK, c), BF16), pltpu.VMEM((ts + BLOCK, c), BF16)],
        compiler_params=pltpu.CompilerParams(
            dimension_semantics=("arbitrary", "arbitrary"), vmem_limit_bytes=_vmem_limit(blk)),
        name="pool_mixer",
    )(p, p, pool_w, pool_scale.reshape(1, c))


def _sb_attn_kernel(q_ref, k_ref, v_ref, o_ref, *, n_blocks, scale):
    row = lax.broadcasted_iota(jnp.int32, (BLOCK, BLOCK), 0)
    col = lax.broadcasted_iota(jnp.int32, (BLOCK, BLOCK), 1)
    causal = col < row
    jj = lax.broadcasted_iota(jnp.int32, (BLOCK, 2 * BLOCK), 0)
    ss = lax.broadcasted_iota(jnp.int32, (BLOCK, 2 * BLOCK), 1)
    tmat = jnp.where((jj > ss) | (ss >= BLOCK), 1.0, 0.0).astype(BF16)

    def tile(q, kj, vj, carry, mask):
        z = lax.dot_general(q, kj, (((1,), (1,)), ((), ())), preferred_element_type=F32) * scale
        sp = jnp.maximum(z, 0.0) + jnp.log1p(jnp.exp(-jnp.abs(z)))
        log_beta = z - sp
        if mask is not None:
            sp = jnp.where(mask, sp, 0.0)
        hi, lo = _split_bf16(sp)
        sums = (jnp.dot(hi, tmat, preferred_element_type=F32)
                + jnp.dot(lo, tmat, preferred_element_type=F32))
        a = jnp.exp(log_beta - sums[:, :BLOCK] + carry)
        if mask is not None:
            a = jnp.where(mask, a, 0.0)
        pv = jnp.dot(a.astype(BF16), vj, preferred_element_type=F32)
        return pv, carry - sums[:, BLOCK:]

    def q_block(qi, _):
        qs = pl.multiple_of(qi * BLOCK, BLOCK)
        q = q_ref[0, pl.ds(qs, BLOCK), :]
        acc, carry = tile(q, k_ref[0, pl.ds(qs, BLOCK), :], v_ref[0, pl.ds(qs, BLOCK), :],
                          jnp.zeros((BLOCK, BLOCK), F32), causal)

        def cond(c):
            j, _, _, top = c
            return jnp.logical_and(j >= 0, top > F32_EXP_UNDERFLOW)

        def body(c):
            j, acc, carry, _ = c
            ks = pl.multiple_of(j * BLOCK, BLOCK)
            pv, carry = tile(q, k_ref[0, pl.ds(ks, BLOCK), :], v_ref[0, pl.ds(ks, BLOCK), :], carry, None)
            return j - 1, acc + pv, carry, jnp.max(carry)

        _, acc, _, _ = lax.while_loop(cond, body, (qi - 1, acc, carry, jnp.max(carry)))
        o_ref[0, pl.ds(qs, BLOCK), :] = acc.astype(o_ref.dtype)
        return 0

    lax.fori_loop(0, n_blocks, q_block, 0)


def _sb_attention(qkv, width):
    b, s, _ = qkv.shape
    heads = width // SB_HEAD_DIM
    blk = 2 * 4 * s * SB_HEAD_DIM * 2
    spec = lambda off: pl.BlockSpec((1, s, SB_HEAD_DIM), lambda bi, h: (bi, 0, off * heads + h))
    return pl.pallas_call(
        functools.partial(_sb_attn_kernel, n_blocks=s // BLOCK, scale=1.0 / math.sqrt(SB_HEAD_DIM)),
        grid=(b, heads),
        in_specs=[spec(0), spec(1), spec(2)],
        out_specs=pl.BlockSpec((1, s, SB_HEAD_DIM), lambda bi, h: (bi, 0, h)),
        out_shape=jax.ShapeDtypeStruct((b, s, width), BF16),
        compiler_params=pltpu.CompilerParams(
            dimension_semantics=("arbitrary", "arbitrary"), vmem_limit_bytes=_vmem_limit(blk)),
        name="sb_attention",
    )(qkv, qkv, qkv)


def _out_proj_ln_kernel(a1_ref, a2_ref, x_ref, w_ref, g_ref, b_ref, o_ref, *, alpha):
    k1 = a1_ref.shape[1]
    y = (jnp.dot(a1_ref[...], w_ref[:k1, :], preferred_element_type=F32)
         + jnp.dot(a2_ref[...], w_ref[k1:, :], preferred_element_type=F32))
    o_ref[...] = _layer_norm(alpha * x_ref[...] + y, g_ref[...], b_ref[...])


def _out_proj_ln(a1, a2, x2, w_out, g, bvec, alpha, tm):
    m, d = x2.shape
    k1, k2 = a1.shape[1], a2.shape[1]
    blk = 2 * tm * (k1 + k2) * 2 + 2 * tm * d * 4 + 2 * (k1 + k2) * d * 2 + 2 * tm * d * 4 + tm * d * 4
    return pl.pallas_call(
        functools.partial(_out_proj_ln_kernel, alpha=alpha),
        grid=(m // tm,),
        in_specs=[pl.BlockSpec((tm, k1), lambda i: (i, 0)),
                  pl.BlockSpec((tm, k2), lambda i: (i, 0)),
                  pl.BlockSpec((tm, d), lambda i: (i, 0)),
                  _const_spec((k1 + k2, d)), _const_spec((1, d)), _const_spec((1, d))],
        out_specs=pl.BlockSpec((tm, d), lambda i: (i, 0)),
        out_shape=jax.ShapeDtypeStruct((m, d), F32),
        compiler_params=pltpu.CompilerParams(
            dimension_semantics=("arbitrary",), vmem_limit_bytes=_vmem_limit(blk)),
        name="out_proj_ln_even",
    )(a1, a2, x2, w_out, g.reshape(1, d), bvec.reshape(1, d))


def _ffn_kernel(h_ref, wg_ref, wu_ref, wd_ref, g_ref, b_ref, o_ref, hb_ref, acc_ref, *, alpha):
    j = pl.program_id(1)

    @pl.when(j == 0)
    def _():
        hb_ref[...] = h_ref[...].astype(BF16)
        acc_ref[...] = jnp.zeros_like(acc_ref)

    hb = hb_ref[...]
    gate = jnp.dot(hb, wg_ref[...], preferred_element_type=F32)
    up = jnp.dot(hb, wu_ref[...], preferred_element_type=F32)
    act = (_silu(gate) * up).astype(BF16)
    acc_ref[...] += jnp.dot(act, wd_ref[...], preferred_element_type=F32)

    @pl.when(j == pl.num_programs(1) - 1)
    def _():
        o_ref[...] = _layer_norm(alpha * h_ref[...] + acc_ref[...], g_ref[...], b_ref[...])


def _ffn_ln(h2, w_gu, w_down, g, bvec, alpha, tm, tf):
    m, d = h2.shape
    f = w_down.shape[0]
    nf = f // tf
    blk = 2 * tm * d * 4 + 2 * 2 * d * tf * 2 + 2 * tf * d * 2 + 2 * tm * d * 4 + tm * d * 2 + tm * d * 4 + 3 * tm * tf * 4
    return pl.pallas_call(
        functools.partial(_ffn_kernel, alpha=alpha),
        grid=(m // tm, nf),
        in_specs=[pl.BlockSpec((tm, d), lambda i, j: (i, 0)),
                  pl.BlockSpec((d, tf), lambda i, j: (0, j)),
                  pl.BlockSpec((d, tf), lambda i, j: (0, j + nf)),
                  pl.BlockSpec((tf, d), lambda i, j: (j, 0)),
                  _const_spec((1, d)), _const_spec((1, d))],
        out_specs=pl.BlockSpec((tm, d), lambda i, j: (i, 0)),
        out_shape=jax.ShapeDtypeStruct((m, d), F32),
        scratch_shapes=[pltpu.VMEM((tm, d), BF16), pltpu.VMEM((tm, d), F32)],
        compiler_params=pltpu.CompilerParams(
            dimension_semantics=("arbitrary", "arbitrary"), vmem_limit_bytes=_vmem_limit(blk)),
        name="ffn_ln_even",
    )(h2, w_gu, w_gu, w_down, g.reshape(1, d), bvec.reshape(1, d))


def _in_proj_odd_kernel(h_ref, w_ref, g_ref, b_ref, u_ref, v_ref):
    hb = h_ref[...].astype(BF16)
    width = u_ref.shape[1]
    u_ref[...] = _gelu_tanh(jnp.dot(hb, w_ref[:, :width], preferred_element_type=F32))
    v = _gelu_tanh(jnp.dot(hb, w_ref[:, width:], preferred_element_type=F32))
    v_ref[...] = _layer_norm(v, g_ref[...], b_ref[...]).astype(BF16)


def _in_proj_odd(h2, w_in, g, bvec, tm):
    m, d = h2.shape
    n = w_in.shape[1]
    width = n // 2
    blk = 2 * tm * d * 4 + 2 * d * n * 2 + 2 * tm * width * 4 + 2 * tm * width * 2 + 2 * tm * width * 4
    return pl.pallas_call(
        _in_proj_odd_kernel,
        grid=(m // tm,),
        in_specs=[pl.BlockSpec((tm, d), lambda i: (i, 0)), _const_spec((d, n)),
                  _const_spec((1, width)), _const_spec((1, width))],
        out_specs=[pl.BlockSpec((tm, width), lambda i: (i, 0)),
                   pl.BlockSpec((tm, width), lambda i: (i, 0))],
        out_shape=[jax.ShapeDtypeStruct((m, width), F32), jax.ShapeDtypeStruct((m, width), BF16)],
        compiler_params=pltpu.CompilerParams(
            dimension_semantics=("arbitrary",), vmem_limit_bytes=_vmem_limit(blk)),
        name="in_proj_odd",
    )(h2, w_in, g.reshape(1, width), bvec.reshape(1, width))


def _sgu_out_ln_kernel(u_ref, v_ref, ws_ref, bs_ref, h_ref, w_ref, g_ref, b_ref, o_ref, gated_ref, *, alpha):
    tm, width = u_ref.shape
    gd = width // SGU_GROUPS
    row = lax.broadcasted_iota(jnp.int32, (SGU_CHUNK, SGU_CHUNK), 0)
    col = lax.broadcasted_iota(jnp.int32, (SGU_CHUNK, SGU_CHUNK), 1)
    lower = col <= row
    for gi in range(SGU_GROUPS):
        ws = jnp.where(lower, ws_ref[gi], 0.0).astype(BF16)
        bias = bs_ref[gi]
        cs = slice(gi * gd, (gi + 1) * gd)
        for n in range(tm // SGU_CHUNK):
            rs = slice(n * SGU_CHUNK, (n + 1) * SGU_CHUNK)
            mixed = jnp.dot(ws, v_ref[rs, cs], preferred_element_type=F32) + bias
            gated_ref[rs, cs] = (u_ref[rs, cs] * mixed).astype(BF16)
    y = jnp.dot(gated_ref[...], w_ref[...], preferred_element_type=F32)
    o_ref[...] = _layer_norm(alpha * h_ref[...] + y, g_ref[...], b_ref[...])


def _sgu_out_ln(u, v, w_s, b_s, h2, w_out, g, bvec, alpha, tm):
    m, d = h2.shape
    width = u.shape[1]
    blk = (2 * tm * width * 4 + 2 * tm * width * 2 + 2 * tm * d * 4 + 2 * width * d * 2 + 2 * tm * d * 4
           + tm * width * 2 + tm * d * 4 + 4 * SGU_GROUPS * SGU_CHUNK * V7X_LANES * 4)
    return pl.pallas_call(
        functools.partial(_sgu_out_ln_kernel, alpha=alpha),
        grid=(m // tm,),
        in_specs=[pl.BlockSpec((tm, width), lambda i: (i, 0)),
                  pl.BlockSpec((tm, width), lambda i: (i, 0)),
                  _const_spec((SGU_GROUPS, SGU_CHUNK, SGU_CHUNK)),
                  _const_spec((SGU_GROUPS, SGU_CHUNK, 1)),
                  pl.BlockSpec((tm, d), lambda i: (i, 0)),
                  _const_spec((width, d)), _const_spec((1, d)), _const_spec((1, d))],
        out_specs=pl.BlockSpec((tm, d), lambda i: (i, 0)),
        out_shape=jax.ShapeDtypeStruct((m, d), F32),
        scratch_shapes=[pltpu.VMEM((tm, width), BF16)],
        compiler_params=pltpu.CompilerParams(
            dimension_semantics=("arbitrary",), vmem_limit_bytes=_vmem_limit(blk)),
        name="sgu_out_ln_odd",
    )(u, v, w_s, b_s.reshape(SGU_GROUPS, SGU_CHUNK, 1), h2, w_out, g.reshape(1, d), bvec.reshape(1, d))


def _top2_gates(logits, n_experts):
    lane = lax.broadcasted_iota(jnp.int32, logits.shape, 1)
    neg = jnp.float32(-jnp.inf)
    lg = jnp.where(lane < n_experts, logits, neg)
    m1 = jnp.max(lg, axis=1, keepdims=True)
    i1 = jnp.min(jnp.where(lg == m1, lane, logits.shape[1]), axis=1, keepdims=True)
    lg2 = jnp.where(lane == i1, neg, lg)
    m2 = jnp.max(lg2, axis=1, keepdims=True)
    i2 = jnp.min(jnp.where(lg2 == m2, lane, logits.shape[1]), axis=1, keepdims=True)
    e2 = jnp.exp(m2 - m1)
    denom = 1.0 + e2
    return jnp.where(lane == i1, 1.0 / denom, 0.0) + jnp.where(lane == i2, e2 / denom, 0.0)


def _moe_kernel(h_ref, wr_ref, wg_ref, wu_ref, wd_ref, g_ref, b_ref, o_ref, hb_ref, acc_ref, gates_ref,
                *, alpha, n_experts):
    e = pl.program_id(1)
    c = pl.program_id(2)

    @pl.when(jnp.logical_and(e == 0, c == 0))
    def _():
        h = h_ref[...]
        hb_ref[...] = h.astype(BF16)
        acc_ref[...] = jnp.zeros_like(acc_ref)
        logits = jnp.dot(h, wr_ref[...], preferred_element_type=F32, precision=lax.Precision.HIGHEST)
        gates_ref[...] = _top2_gates(logits, n_experts)

    hb = hb_ref[...]
    gate = jnp.dot(hb, wg_ref[0], preferred_element_type=F32)
    up = jnp.dot(hb, wu_ref[0], preferred_element_type=F32)
    act = (_silu(gate) * up).astype(BF16)
    y = jnp.dot(act, wd_ref[0], preferred_element_type=F32)
    lane = lax.broadcasted_iota(jnp.int32, gates_ref.shape, 1)
    ge = jnp.sum(jnp.where(lane == e, gates_ref[...], 0.0), axis=1, keepdims=True)
    acc_ref[...] += ge * y

    @pl.when(jnp.logical_and(e == pl.num_programs(1) - 1, c == pl.num_programs(2) - 1))
    def _():
        o_ref[...] = _layer_norm(alpha * h_ref[...] + acc_ref[...], g_ref[...], b_ref[...])


def _moe_ln(h2, w_router, w_g, w_u, w_d, g, bvec, alpha, tm, tf):
    m, d = h2.shape
    n_experts, _, fe = w_g.shape
    nc = fe // tf
    blk = (2 * tm * d * 4 + 2 * d * V7X_LANES * 4 + 2 * 2 * d * tf * 2 + 2 * tf * d * 2 + 2 * tm * d * 4
           + tm * d * 2 + tm * d * 4 + tm * V7X_LANES * 4 + 3 * tm * tf * 4 + tm * d * 4)
    return pl.pallas_call(
        functools.partial(_moe_kernel, alpha=alpha, n_experts=n_experts),
        grid=(m // tm, n_experts, nc),
        in_specs=[pl.BlockSpec((tm, d), lambda i, e, c: (i, 0)),
                  _const_spec((d, V7X_LANES)),
                  pl.BlockSpec((1, d, tf), lambda i, e, c: (e, 0, c)),
                  pl.BlockSpec((1, d, tf), lambda i, e, c: (e, 0, c)),
                  pl.BlockSpec((1, tf, d), lambda i, e, c: (e, c, 0)),
                  _const_spec((1, d)), _const_spec((1, d))],
        out_specs=pl.BlockSpec((tm, d), lambda i, e, c: (i, 0)),
        out_shape=jax.ShapeDtypeStruct((m, d), F32),
        scratch_shapes=[pltpu.VMEM((tm, d), BF16), pltpu.VMEM((tm, d), F32), pltpu.VMEM((tm, V7X_LANES), F32)],
        compiler_params=pltpu.CompilerParams(
            dimension_semantics=("arbitrary", "arbitrary", "arbitrary"), vmem_limit_bytes=_vmem_limit(blk)),
        name="moe_ln_odd",
    )(h2, w_router, w_g, w_u, w_d, g.reshape(1, d), bvec.reshape(1, d))


def _pad_to(x, axis, mult):
    pad = (-x.shape[axis]) % mult
    if pad == 0:
        return x
    widths = [(0, 0)] * x.ndim
    widths[axis] = (0, pad)
    return jnp.pad(x, widths)


def kernel(x, even_w_in, even_pool_w, even_pool_scale, even_w_out, even_ln1_g, even_ln1_b, even_ffn_w_gu, even_ffn_w_down, even_ln2_g, even_ln2_b, odd_w_in, odd_sgu_ln_g, odd_sgu_ln_b, odd_sgu_w, odd_sgu_b, odd_w_out, odd_ln1_g, odd_ln1_b, odd_router, odd_moe_w_gu, odd_moe_w_down, odd_ln2_g, odd_ln2_b):
    b, s, d = x.shape
    depth = even_w_in.shape[0] + odd_w_in.shape[0]
    alpha = (2.0 * depth) ** 0.25
    pool_width = even_pool_scale.shape[1]
    sb_width = (even_w_in.shape[2] - pool_width) // 3
    m = b * s
    tm = min(512, m)
    moe_tf = 512

    h = x.reshape(m, d)
    for layer in range(depth):
        i = layer // 2
        if layer % 2 == 0:
            p, qkv = _in_proj_even(h, even_w_in[i].astype(BF16), pool_width, tm)
            pooled = _pool_mixer(p.reshape(b, s, pool_width), even_pool_w[i].astype(BF16),
                                 even_pool_scale[i], min(512, s))
            attn = _sb_attention(qkv.reshape(b, s, 3 * sb_width), sb_width)
            h = _out_proj_ln(pooled.reshape(m, pool_width), attn.reshape(m, sb_width), h,
                             even_w_out[i].astype(BF16), even_ln1_g[i], even_ln1_b[i], alpha, tm)
            h = _ffn_ln(h, even_ffn_w_gu[i].astype(BF16), even_ffn_w_down[i].astype(BF16),
                        even_ln2_g[i], even_ln2_b[i], alpha, tm, 512)
        else:
            u, v = _in_proj_odd(h, odd_w_in[i].astype(BF16), odd_sgu_ln_g[i], odd_sgu_ln_b[i], tm)
            h = _sgu_out_ln(u, v, odd_sgu_w[i], odd_sgu_b[i], h, odd_w_out[i].astype(BF16),
                            odd_ln1_g[i], odd_ln1_b[i], alpha, tm)
            fe = odd_moe_w_down.shape[2]
            w_g = _pad_to(odd_moe_w_gu[i][:, :, :fe].astype(BF16), 2, moe_tf)
            w_u = _pad_to(odd_moe_w_gu[i][:, :, fe:].astype(BF16), 2, moe_tf)
            w_d = _pad_to(odd_moe_w_down[i].astype(BF16), 1, moe_tf)
            w_r = _pad_to(odd_router[i], 1, V7X_LANES)
            h = _moe_ln(h, w_r, w_g, w_u, w_d, odd_ln2_g[i], odd_ln2_b[i], alpha, tm, moe_tf)
    return h.reshape(b, s, d)
```

```python
import functools
import math

import jax
import jax.numpy as jnp
from jax import lax
from jax.experimental import pallas as pl
from jax.experimental.pallas import tpu as pltpu

F32 = jnp.float32
BF16 = jnp.bfloat16

LN_EPS = 1e-5
POOL_WINDOWS = (2, 4, 8, 16)
SB_HEAD_DIM = 128
SGU_GROUPS = 8
SGU_CHUNK = 128
TOP_K = 2

V7X_LANES = 128
V7X_VMEM_BYTES = 64 * 1024 * 1024
BLOCK = 128

F32_EXP_UNDERFLOW = -104.0
NO_WEIGHT = -1e30


def _vmem_limit(block_bytes):
    return int(min(block_bytes * 1.3 + (6 << 20), V7X_VMEM_BYTES - (6 << 20)))


def _layer_norm(r, g, b):
    mu = jnp.mean(r, axis=-1, keepdims=True)
    xc = r - mu
    var = jnp.mean(xc * xc, axis=-1, keepdims=True)
    return xc * lax.rsqrt(var + LN_EPS) * g + b


def _silu(x):
    return x * jax.nn.sigmoid(x)


def _gelu_tanh(x):
    c = math.sqrt(2.0 / math.pi)
    return 0.5 * x * (1.0 + jnp.tanh(c * (x + 0.044715 * (x * x * x))))


def _split_bf16(x):
    hi = x.astype(BF16)
    lo = (x - hi.astype(F32)).astype(BF16)
    return hi, lo


def _const_spec(shape):
    nd = len(shape)
    return pl.BlockSpec(shape, lambda *_: (0,) * nd)


def _pad_to(x, axis, mult):
    pad = (-x.shape[axis]) % mult
    if pad == 0:
        return x
    widths = [(0, 0)] * x.ndim
    widths[axis] = (0, pad)
    return jnp.pad(x, widths)


def _in_proj_even_kernel(x_ref, w_ref, p_ref, qkv_ref, *, pool_width):
    xb = x_ref[...].astype(BF16)
    p_ref[...] = jnp.dot(xb, w_ref[:, :pool_width], preferred_element_type=F32)
    n_rest = (w_ref.shape[1] - pool_width) // pool_width
    for c in range(n_rest):
        lo = pool_width * (c + 1)
        qkv_ref[:, c * pool_width:(c + 1) * pool_width] = jnp.dot(
            xb, w_ref[:, lo:lo + pool_width], preferred_element_type=F32).astype(BF16)


def _in_proj_even(x2, w_in, pool_width, tm):
    m, d = x2.shape
    n = w_in.shape[1]
    blk = 2 * tm * d * 4 + 2 * d * n * 2 + 2 * tm * pool_width * 4 + 2 * tm * (n - pool_width) * 2 + tm * n * 4
    return pl.pallas_call(
        functools.partial(_in_proj_even_kernel, pool_width=pool_width),
        grid=(m // tm,),
        in_specs=[pl.BlockSpec((tm, d), lambda i: (i, 0)), _const_spec((d, n))],
        out_specs=[pl.BlockSpec((tm, pool_width), lambda i: (i, 0)),
                   pl.BlockSpec((tm, n - pool_width), lambda i: (i, 0))],
        out_shape=[jax.ShapeDtypeStruct((m, pool_width), F32),
                   jax.ShapeDtypeStruct((m, n - pool_width), BF16)],
        compiler_params=pltpu.CompilerParams(
            dimension_semantics=("arbitrary",), vmem_limit_bytes=_vmem_limit(blk)),
        name="in_proj_even",
    )(x2, w_in)


def _pool_kernel(pc_ref, pp_ref, w_ref, sc_ref, o_ref, hi_ref, lo_ref, *, ts, group_dim):
    i = pl.program_id(1)
    prev = jnp.where(i > 0, pp_ref[0], 0.0)
    h, l = _split_bf16(prev)
    hi_ref[0:BLOCK, :] = h
    lo_ref[0:BLOCK, :] = l
    h, l = _split_bf16(pc_ref[0])
    hi_ref[BLOCK:, :] = h
    lo_ref[BLOCK:, :] = l

    ii = lax.broadcasted_iota(jnp.int32, (BLOCK, 2 * BLOCK), 0) + BLOCK
    jj = lax.broadcasted_iota(jnp.int32, (BLOCK, 2 * BLOCK), 1)
    row = lax.broadcasted_iota(jnp.int32, (BLOCK, 1), 0)
    for g, w in enumerate(POOL_WINDOWS):
        band = jnp.where((jj <= ii) & (jj > ii - w), 1.0, 0.0).astype(BF16)
        cs = slice(g * group_dim, (g + 1) * group_dim)
        for r in range(ts // BLOCK):
            rs = slice(r * BLOCK, r * BLOCK + 2 * BLOCK)
            wsum = (jnp.dot(band, hi_ref[rs, cs], preferred_element_type=F32)
                    + jnp.dot(band, lo_ref[rs, cs], preferred_element_type=F32))
            t = i * ts + r * BLOCK + row
            count = jnp.minimum(t + 1, w).astype(F32)
            d = wsum / count - pc_ref[0, r * BLOCK:(r + 1) * BLOCK, cs]
            y = jnp.dot(d.astype(BF16), w_ref[g], preferred_element_type=F32) * sc_ref[:, cs]
            o_ref[0, r * BLOCK:(r + 1) * BLOCK, cs] = y.astype(o_ref.dtype)


def _pool_mixer(p, pool_w, pool_scale, ts):
    b, s, c = p.shape
    groups = len(POOL_WINDOWS)
    gd = c // groups
    per = ts // BLOCK
    blk = 2 * ts * c * 4 + 2 * BLOCK * c * 4 + 2 * groups * gd * gd * 2 + 2 * ts * c * 2 + 2 * (ts + BLOCK) * c * 2
    return pl.pallas_call(
        functools.partial(_pool_kernel, ts=ts, group_dim=gd),
        grid=(b, s // ts),
        in_specs=[pl.BlockSpec((1, ts, c), lambda bi, i: (bi, i, 0)),
                  pl.BlockSpec((1, BLOCK, c), lambda bi, i: (bi, jnp.maximum(i * per - 1, 0), 0)),
                  _const_spec((groups, gd, gd)),
                  _const_spec((1, c))],
        out_specs=pl.BlockSpec((1, ts, c), lambda bi, i: (bi, i, 0)),
        out_shape=jax.ShapeDtypeStruct((b, s, c), BF16),
        scratch_shapes=[pltpu.VMEM((ts + BLOCK, c), BF16), pltpu.VMEM((ts + BLOCK, c), BF16)],
        compiler_params=pltpu.CompilerParams(
            dimension_semantics=("arbitrary", "arbitrary"), vmem_limit_bytes=_vmem_limit(blk)),
        name="pool_mixer",
    )(p, p, pool_w, pool_scale.reshape(1, c))


def _sb_attn_kernel(q_ref, k_ref, v_ref, o_ref, *scratch, n_blocks, group, scale):
    acc_refs, carry_refs = scratch[:group], scratch[group:]
    row = lax.broadcasted_iota(jnp.int32, (BLOCK, BLOCK), 0)
    col = lax.broadcasted_iota(jnp.int32, (BLOCK, BLOCK), 1)
    causal = col < row
    jj = lax.broadcasted_iota(jnp.int32, (BLOCK, 2 * BLOCK), 0)
    ss = lax.broadcasted_iota(jnp.int32, (BLOCK, 2 * BLOCK), 1)
    tmat = jnp.where((jj > ss) | (ss >= BLOCK), 1.0, 0.0).astype(BF16)

    def tiles(qs, ks, vs, carries, mask):
        n = len(qs)
        log_betas, his, los = [], [], []
        for q, kj in zip(qs, ks):
            z = lax.dot_general(q, kj, (((1,), (1,)), ((), ())), preferred_element_type=F32) * scale
            sp = jnp.maximum(z, 0.0) + jnp.log1p(jnp.exp(-jnp.abs(z)))
            log_betas.append(z - sp)
            if mask is not None:
                sp = jnp.where(mask, sp, 0.0)
            hi, lo = _split_bf16(sp)
            his.append(hi)
            los.append(lo)
        sums = jnp.dot(jnp.concatenate(his + los, axis=0), tmat, preferred_element_type=F32)
        pvs, new_carries = [], []
        for r in range(n):
            s = sums[r * BLOCK:(r + 1) * BLOCK] + sums[(n + r) * BLOCK:(n + r + 1) * BLOCK]
            a = jnp.exp(log_betas[r] - s[:, :BLOCK] + carries[r])
            if mask is not None:
                a = jnp.where(mask, a, 0.0)
            pvs.append(jnp.dot(a.astype(BF16), vs[r], preferred_element_type=F32))
            new_carries.append(carries[r] - s[:, BLOCK:])
        return pvs, new_carries

    def top_of(carries):
        m = carries[0]
        for c in carries[1:]:
            m = jnp.maximum(m, c)
        return jnp.max(m)

    def q_group(gi, _):
        base = gi * group
        offs = [pl.multiple_of((base + r) * BLOCK, BLOCK) for r in range(group)]
        load = lambda ref, off: ref[0, pl.ds(off, BLOCK), :]
        pvs, carries = tiles([load(q_ref, o) for o in offs], [load(k_ref, o) for o in offs],
                             [load(v_ref, o) for o in offs], [jnp.zeros((BLOCK, BLOCK), F32)] * group, causal)
        for r in range(group):
            acc_refs[r][...] = pvs[r]
            carry_refs[r][...] = carries[r]

        def cond(c):
            k, top = c
            return jnp.logical_and(k < base + group, top > F32_EXP_UNDERFLOW)

        def body(c):
            k, _ = c
            koffs, carries = [], []
            for r in range(group):
                j = base + r - k
                carries.append(jnp.where(j >= 0, carry_refs[r][...], NO_WEIGHT))
                koffs.append(pl.multiple_of(jnp.maximum(j, 0) * BLOCK, BLOCK))
            pvs, carries = tiles([load(q_ref, o) for o in offs], [load(k_ref, o) for o in koffs],
                                 [load(v_ref, o) for o in koffs], carries, None)
            for r in range(group):
                acc_refs[r][...] += pvs[r]
                carry_refs[r][...] = carries[r]
            return k + 1, top_of(carries)

        lax.while_loop(cond, body, (jnp.int32(1), top_of(carries)))
        for r in range(group):
            o_ref[0, pl.ds(offs[r], BLOCK), :] = acc_refs[r][...].astype(o_ref.dtype)
        return 0

    lax.fori_loop(0, n_blocks // group, q_group, 0)


def _sb_attention(qkv, width):
    b, s, _ = qkv.shape
    heads = width // SB_HEAD_DIM
    n_blocks = s // BLOCK
    group = math.gcd(n_blocks, 8)
    blk = 2 * 4 * s * SB_HEAD_DIM * 2 + 2 * group * BLOCK * BLOCK * 4
    spec = lambda off: pl.BlockSpec((1, s, SB_HEAD_DIM), lambda bi, h: (bi, 0, off * heads + h))
    return pl.pallas_call(
        functools.partial(_sb_attn_kernel, n_blocks=n_blocks, group=group, scale=1.0 / math.sqrt(SB_HEAD_DIM)),
        grid=(b, heads),
        in_specs=[spec(0), spec(1), spec(2)],
        out_specs=pl.BlockSpec((1, s, SB_HEAD_DIM), lambda bi, h: (bi, 0, h)),
        out_shape=jax.ShapeDtypeStruct((b, s, width), BF16),
        scratch_shapes=[pltpu.VMEM((BLOCK, BLOCK), F32)] * (2 * group),
        compiler_params=pltpu.CompilerParams(
            dimension_semantics=("arbitrary", "arbitrary"), vmem_limit_bytes=_vmem_limit(blk)),
        name="sb_attention",
    )(qkv, qkv, qkv)


def _out_proj_ln_kernel(a1_ref, a2_ref, x_ref, w_ref, g_ref, b_ref, o_ref, *, alpha):
    k1 = a1_ref.shape[1]
    y = (jnp.dot(a1_ref[...], w_ref[:k1, :], preferred_element_type=F32)
         + jnp.dot(a2_ref[...], w_ref[k1:, :], preferred_element_type=F32))
    o_ref[...] = _layer_norm(alpha * x_ref[...] + y, g_ref[...], b_ref[...])


def _out_proj_ln(a1, a2, x2, w_out, g, bvec, alpha, tm):
    m, d = x2.shape
    k1, k2 = a1.shape[1], a2.shape[1]
    blk = 2 * tm * (k1 + k2) * 2 + 2 * tm * d * 4 + 2 * (k1 + k2) * d * 2 + 2 * tm * d * 4 + tm * d * 4
    return pl.pallas_call(
        functools.partial(_out_proj_ln_kernel, alpha=alpha),
        grid=(m // tm,),
        in_specs=[pl.BlockSpec((tm, k1), lambda i: (i, 0)),
                  pl.BlockSpec((tm, k2), lambda i: (i, 0)),
                  pl.BlockSpec((tm, d), lambda i: (i, 0)),
                  _const_spec((k1 + k2, d)), _const_spec((1, d)), _const_spec((1, d))],
        out_specs=pl.BlockSpec((tm, d), lambda i: (i, 0)),
        out_shape=jax.ShapeDtypeStruct((m, d), F32),
        compiler_params=pltpu.CompilerParams(
            dimension_semantics=("arbitrary",), vmem_limit_bytes=_vmem_limit(blk)),
        name="out_proj_ln_even",
    )(a1, a2, x2, w_out, g.reshape(1, d), bvec.reshape(1, d))


def _ffn_kernel(h_ref, wg_ref, wu_ref, wd_ref, g_ref, b_ref, o_ref, hb_ref, acc_ref, *, alpha):
    j = pl.program_id(1)

    @pl.when(j == 0)
    def _():
        hb_ref[...] = h_ref[...].astype(BF16)
        acc_ref[...] = jnp.zeros_like(acc_ref)

    hb = hb_ref[...]
    gate = jnp.dot(hb, wg_ref[...], preferred_element_type=F32)
    up = jnp.dot(hb, wu_ref[...], preferred_element_type=F32)
    act = (_silu(gate) * up).astype(BF16)
    acc_ref[...] += jnp.dot(act, wd_ref[...], preferred_element_type=F32)

    @pl.when(j == pl.num_programs(1) - 1)
    def _():
        o_ref[...] = _layer_norm(alpha * h_ref[...] + acc_ref[...], g_ref[...], b_ref[...])


def _ffn_ln(h2, w_gu, w_down, g, bvec, alpha, tm, tf):
    m, d = h2.shape
    f = w_down.shape[0]
    nf = f // tf
    blk = 2 * tm * d * 4 + 2 * 2 * d * tf * 2 + 2 * tf * d * 2 + 2 * tm * d * 4 + tm * d * 2 + tm * d * 4 + 3 * tm * tf * 4
    return pl.pallas_call(
        functools.partial(_ffn_kernel, alpha=alpha),
        grid=(m // tm, nf),
        in_specs=[pl.BlockSpec((tm, d), lambda i, j: (i, 0)),
                  pl.BlockSpec((d, tf), lambda i, j: (0, j)),
                  pl.BlockSpec((d, tf), lambda i, j: (0, j + nf)),
                  pl.BlockSpec((tf, d), lambda i, j: (j, 0)),
                  _const_spec((1, d)), _const_spec((1, d))],
        out_specs=pl.BlockSpec((tm, d), lambda i, j: (i, 0)),
        out_shape=jax.ShapeDtypeStruct((m, d), F32),
        scratch_shapes=[pltpu.VMEM((tm, d), BF16), pltpu.VMEM((tm, d), F32)],
        compiler_params=pltpu.CompilerParams(
            dimension_semantics=("arbitrary", "arbitrary"), vmem_limit_bytes=_vmem_limit(blk)),
        name="ffn_ln_even",
    )(h2, w_gu, w_gu, w_down, g.reshape(1, d), bvec.reshape(1, d))


def _in_proj_odd_kernel(h_ref, w_ref, g_ref, b_ref, u_ref, v_ref):
    hb = h_ref[...].astype(BF16)
    width = u_ref.shape[1]
    u_ref[...] = _gelu_tanh(jnp.dot(hb, w_ref[:, :width], preferred_element_type=F32))
    v = _gelu_tanh(jnp.dot(hb, w_ref[:, width:], preferred_element_type=F32))
    v_ref[...] = _layer_norm(v, g_ref[...], b_ref[...]).astype(BF16)


def _in_proj_odd(h2, w_in, g, bvec, tm):
    m, d = h2.shape
    n = w_in.shape[1]
    width = n // 2
    blk = 2 * tm * d * 4 + 2 * d * n * 2 + 2 * tm * width * 4 + 2 * tm * width * 2 + 2 * tm * width * 4
    return pl.pallas_call(
        _in_proj_odd_kernel,
        grid=(m // tm,),
        in_specs=[pl.BlockSpec((tm, d), lambda i: (i, 0)), _const_spec((d, n)),
                  _const_spec((1, width)), _const_spec((1, width))],
        out_specs=[pl.BlockSpec((tm, width), lambda i: (i, 0)),
                   pl.BlockSpec((tm, width), lambda i: (i, 0))],
        out_shape=[jax.ShapeDtypeStruct((m, width), F32), jax.ShapeDtypeStruct((m, width), BF16)],
        compiler_params=pltpu.CompilerParams(
            dimension_semantics=("arbitrary",), vmem_limit_bytes=_vmem_limit(blk)),
        name="in_proj_odd",
    )(h2, w_in, g.reshape(1, width), bvec.reshape(1, width))


def _sgu_out_ln_kernel(u_ref, v_ref, ws_ref, bs_ref, h_ref, w_ref, g_ref, b_ref, o_ref, gated_ref, *, alpha):
    tm, width = u_ref.shape
    gd = width // SGU_GROUPS
    row = lax.broadcasted_iota(jnp.int32, (SGU_CHUNK, SGU_CHUNK), 0)
    col = lax.broadcasted_iota(jnp.int32, (SGU_CHUNK, SGU_CHUNK), 1)
    lower = col <= row
    for gi in range(SGU_GROUPS):
        ws = jnp.where(lower, ws_ref[gi], 0.0).astype(BF16)
        bias = bs_ref[gi]
        cs = slice(gi * gd, (gi + 1) * gd)
        for n in range(tm // SGU_CHUNK):
            rs = slice(n * SGU_CHUNK, (n + 1) * SGU_CHUNK)
            mixed = jnp.dot(ws, v_ref[rs, cs], preferred_element_type=F32) + bias
            gated_ref[rs, cs] = (u_ref[rs, cs] * mixed).astype(BF16)
    y = jnp.dot(gated_ref[...], w_ref[...], preferred_element_type=F32)
    o_ref[...] = _layer_norm(alpha * h_ref[...] + y, g_ref[...], b_ref[...])


def _sgu_out_ln(u, v, w_s, b_s, h2, w_out, g, bvec, alpha, tm):
    m, d = h2.shape
    width = u.shape[1]
    blk = (2 * tm * width * 4 + 2 * tm * width * 2 + 2 * tm * d * 4 + 2 * width * d * 2 + 2 * tm * d * 4
           + tm * width * 2 + tm * d * 4 + 4 * SGU_GROUPS * SGU_CHUNK * V7X_LANES * 4)
    return pl.pallas_call(
        functools.partial(_sgu_out_ln_kernel, alpha=alpha),
        grid=(m // tm,),
        in_specs=[pl.BlockSpec((tm, width), lambda i: (i, 0)),
                  pl.BlockSpec((tm, width), lambda i: (i, 0)),
                  _const_spec((SGU_GROUPS, SGU_CHUNK, SGU_CHUNK)),
                  _const_spec((SGU_GROUPS, SGU_CHUNK, 1)),
                  pl.BlockSpec((tm, d), lambda i: (i, 0)),
                  _const_spec((width, d)), _const_spec((1, d)), _const_spec((1, d))],
        out_specs=pl.BlockSpec((tm, d), lambda i: (i, 0)),
        out_shape=jax.ShapeDtypeStruct((m, d), F32),
        scratch_shapes=[pltpu.VMEM((tm, width), BF16)],
        compiler_params=pltpu.CompilerParams(
            dimension_semantics=("arbitrary",), vmem_limit_bytes=_vmem_limit(blk)),
        name="sgu_out_ln_odd",
    )(u, v, w_s, b_s.reshape(SGU_GROUPS, SGU_CHUNK, 1), h2, w_out, g.reshape(1, d), bvec.reshape(1, d))


META_W1, META_W2, META_E1, META_E2, META_R1, META_R2 = range(6)
PLAN_START, PLAN_FILL_BEGIN, PLAN_FILL_END = range(3)


def _route_kernel(h_ref, wr_ref, meta_ref, post_ref, cnt_ref, run_ref, *, n_experts):
    @pl.when(pl.program_id(0) == 0)
    def _():
        run_ref[...] = jnp.zeros_like(run_ref)

    tm = h_ref.shape[0]
    logits = jnp.dot(h_ref[...], wr_ref[...], preferred_element_type=F32, precision=lax.Precision.HIGHEST)
    lanes = logits.shape[1]
    lane = lax.broadcasted_iota(jnp.int32, logits.shape, 1)
    neg = jnp.float32(-jnp.inf)
    lg = jnp.where(lane < n_experts, logits, neg)
    m1 = jnp.max(lg, axis=1, keepdims=True)
    i1 = jnp.min(jnp.where(lg == m1, lane, lanes), axis=1, keepdims=True)
    lg2 = jnp.where(lane == i1, neg, lg)
    m2 = jnp.max(lg2, axis=1, keepdims=True)
    i2 = jnp.min(jnp.where(lg2 == m2, lane, lanes), axis=1, keepdims=True)
    e2 = jnp.exp(m2 - m1)
    denom = 1.0 + e2
    w1 = 1.0 / denom
    w2 = e2 / denom

    onehot = jnp.where((lane == i1) | (lane == i2), 1.0, 0.0)
    row = lax.broadcasted_iota(jnp.int32, (tm, tm), 0)
    col = lax.broadcasted_iota(jnp.int32, (tm, tm), 1)
    earlier = jnp.where(col < row, 1.0, 0.0).astype(BF16)
    rank = jnp.dot(earlier, onehot.astype(BF16), preferred_element_type=F32) + run_ref[...]
    r1 = jnp.sum(jnp.where(lane == i1, rank, 0.0), axis=1, keepdims=True)
    r2 = jnp.sum(jnp.where(lane == i2, rank, 0.0), axis=1, keepdims=True)
    run_ref[...] += jnp.sum(onehot, axis=0, keepdims=True)
    cnt_ref[...] = run_ref[...]

    fields = {META_W1: w1, META_W2: w2, META_E1: i1.astype(F32), META_E2: i2.astype(F32), META_R1: r1, META_R2: r2}
    meta = jnp.zeros_like(logits)
    for k, val in fields.items():
        meta = jnp.where(lane == k, val, meta)
    meta_ref[...] = meta
    post_ref[...] = meta.T[:post_ref.shape[0], :].astype(jnp.int32)


def _moe_route(h2, w_router, tm):
    m, d = h2.shape
    n_experts = w_router.shape[1]
    w_r = _pad_to(w_router, 1, V7X_LANES)
    lanes = w_r.shape[1]
    blk = 2 * tm * d * 4 + 2 * d * lanes * 4 + 4 * tm * lanes * 4 + tm * tm * 6 + 8 * tm * lanes * 4
    return pl.pallas_call(
        functools.partial(_route_kernel, n_experts=n_experts),
        grid=(m // tm,),
        in_specs=[pl.BlockSpec((tm, d), lambda i: (i, 0)), _const_spec((d, lanes))],
        out_specs=[pl.BlockSpec((tm, lanes), lambda i: (i, 0)),
                   pl.BlockSpec((8, tm), lambda i: (0, i)),
                   _const_spec((1, lanes))],
        out_shape=[jax.ShapeDtypeStruct((m, lanes), F32),
                   jax.ShapeDtypeStruct((8, m), jnp.int32),
                   jax.ShapeDtypeStruct((1, lanes), F32)],
        scratch_shapes=[pltpu.VMEM((1, lanes), F32)],
        compiler_params=pltpu.CompilerParams(
            dimension_semantics=("arbitrary",), vmem_limit_bytes=_vmem_limit(blk)),
        name="moe_route",
    )(h2, w_r)


def _row_copy(src_ref, src_row, dst_ref, dst_row, sem):
    return pltpu.make_async_copy(src_ref.at[pl.ds(src_row, 1), :], dst_ref.at[pl.ds(dst_row, 1), :], sem)


def _slot_rows(plan_ref, post_ref, r):
    return (plan_ref[PLAN_START, post_ref[META_E1, r]] + post_ref[META_R1, r],
            plan_ref[PLAN_START, post_ref[META_E2, r]] + post_ref[META_R2, r])


def _dispatch_kernel(plan_ref, post_ref, h_ref, xs_ref, zero_ref, sem):
    tm = h_ref.shape[0]

    def copies(r):
        return [_row_copy(h_ref, r, xs_ref, row, sem) for row in _slot_rows(plan_ref, post_ref, r)]

    def start(r, _):
        for c in copies(r):
            c.start()
        return 0

    def wait(r, _):
        for c in copies(r):
            c.wait()
        return 0

    lax.fori_loop(0, tm, start, 0)
    lax.fori_loop(0, tm, wait, 0)

    @pl.when(pl.program_id(0) == pl.num_programs(0) - 1)
    def _():
        zero_ref[...] = jnp.zeros_like(zero_ref)
        for e in range(plan_ref.shape[1]):
            begin, end = plan_ref[PLAN_FILL_BEGIN, e], plan_ref[PLAN_FILL_END, e]

            def fill_start(r, _):
                _row_copy(zero_ref, 0, xs_ref, r, sem).start()
                return 0

            def fill_wait(r, _):
                _row_copy(zero_ref, 0, xs_ref, r, sem).wait()
                return 0

            lax.fori_loop(begin, end, fill_start, 0)
            lax.fori_loop(begin, end, fill_wait, 0)


def _moe_dispatch(h2, post, plan, rows, tm):
    m, d = h2.shape
    blk = 2 * tm * d * 4 + 8 * d * 4
    return pl.pallas_call(
        _dispatch_kernel,
        grid=(m // tm,),
        in_specs=[pl.BlockSpec(memory_space=pltpu.SMEM),
                  pl.BlockSpec((8, tm), lambda i: (0, i), memory_space=pltpu.SMEM),
                  pl.BlockSpec((tm, d), lambda i: (i, 0))],
        out_specs=pl.BlockSpec(memory_space=pl.ANY),
        out_shape=jax.ShapeDtypeStruct((rows, d), F32),
        scratch_shapes=[pltpu.VMEM((8, d), F32), pltpu.SemaphoreType.DMA(())],
        compiler_params=pltpu.CompilerParams(
            dimension_semantics=("arbitrary",), vmem_limit_bytes=_vmem_limit(blk)),
        name="moe_dispatch",
    )(plan, post, h2)


def _experts_kernel(te_ref, nt_ref, xs_ref, wgu_ref, wd_ref, ys_ref):
    occupied = pl.program_id(0) < nt_ref[0]

    @pl.when(occupied)
    def _():
        fe = wd_ref.shape[1]
        xb = xs_ref[...].astype(BF16)
        gu = jnp.dot(xb, wgu_ref[0], preferred_element_type=F32)
        act = (_silu(gu[:, :fe]) * gu[:, fe:]).astype(BF16)
        ys_ref[...] = jnp.dot(act, wd_ref[0], preferred_element_type=F32)

    @pl.when(jnp.logical_not(occupied))
    def _():
        ys_ref[...] = jnp.zeros_like(ys_ref)


def _moe_experts(xs, w_gu, w_d, tile_expert, n_tiles, tile):
    rows, d = xs.shape
    _, _, fe2 = w_gu.shape
    fe = w_d.shape[1]
    blk = 2 * tile * d * 4 * 2 + 2 * d * fe2 * 2 + 2 * fe * d * 2 + tile * fe2 * 4 + tile * fe * 6 + tile * d * 2
    grid_spec = pltpu.PrefetchScalarGridSpec(
        num_scalar_prefetch=2,
        grid=(rows // tile,),
        in_specs=[pl.BlockSpec((tile, d), lambda i, te, nt: (i, 0)),
                  pl.BlockSpec((1, d, fe2), lambda i, te, nt: (te[i], 0, 0)),
                  pl.BlockSpec((1, fe, d), lambda i, te, nt: (te[i], 0, 0))],
        out_specs=pl.BlockSpec((tile, d), lambda i, te, nt: (i, 0)),
    )
    return pl.pallas_call(
        _experts_kernel,
        grid_spec=grid_spec,
        out_shape=jax.ShapeDtypeStruct((rows, d), F32),
        compiler_params=pltpu.CompilerParams(
            dimension_semantics=("arbitrary",), vmem_limit_bytes=_vmem_limit(blk)),
        name="moe_experts",
    )(tile_expert, n_tiles, xs, w_gu, w_d)


def _combine_ln_kernel(plan_ref, post_ref, h_ref, meta_ref, ys_ref, g_ref, b_ref, o_ref, y1_ref, y2_ref, sem,
                       *, alpha):
    tm = h_ref.shape[0]

    def copies(r):
        row1, row2 = _slot_rows(plan_ref, post_ref, r)
        return [_row_copy(ys_ref, row1, y1_ref, r, sem), _row_copy(ys_ref, row2, y2_ref, r, sem)]

    def start(r, _):
        for c in copies(r):
            c.start()
        return 0

    def wait(r, _):
        for c in copies(r):
            c.wait()
        return 0

    lax.fori_loop(0, tm, start, 0)
    lax.fori_loop(0, tm, wait, 0)
    meta = meta_ref[...]
    moe = meta[:, META_W1:META_W1 + 1] * y1_ref[...] + meta[:, META_W2:META_W2 + 1] * y2_ref[...]
    o_ref[...] = _layer_norm(alpha * h_ref[...] + moe, g_ref[...], b_ref[...])


def _moe_combine_ln(h2, post, plan, meta, ys, g, bvec, alpha, tm):
    m, d = h2.shape
    lanes = meta.shape[1]
    blk = 2 * tm * d * 4 * 2 + 2 * tm * lanes * 4 + 2 * tm * d * 4 + 2 * tm * d * 4
    return pl.pallas_call(
        functools.partial(_combine_ln_kernel, alpha=alpha),
        grid=(m // tm,),
        in_specs=[pl.BlockSpec(memory_space=pltpu.SMEM),
                  pl.BlockSpec((8, tm), lambda i: (0, i), memory_space=pltpu.SMEM),
                  pl.BlockSpec((tm, d), lambda i: (i, 0)),
                  pl.BlockSpec((tm, lanes), lambda i: (i, 0)),
                  pl.BlockSpec(memory_space=pl.ANY),
                  _const_spec((1, d)), _const_spec((1, d))],
        out_specs=pl.BlockSpec((tm, d), lambda i: (i, 0)),
        out_shape=jax.ShapeDtypeStruct((m, d), F32),
        scratch_shapes=[pltpu.VMEM((tm, d), F32), pltpu.VMEM((tm, d), F32), pltpu.SemaphoreType.DMA(())],
        compiler_params=pltpu.CompilerParams(
            dimension_semantics=("arbitrary",), vmem_limit_bytes=_vmem_limit(blk)),
        name="moe_combine_ln",
    )(plan, post, h2, meta, ys, g.reshape(1, d), bvec.reshape(1, d))


def _moe_ln(h2, w_router, w_gu, w_d, g, bvec, alpha, tm, tile):
    m, d = h2.shape
    n_experts = w_router.shape[1]
    meta, post, counts = _moe_route(h2, w_router, tm)
    cnt = counts[0, :n_experts].astype(jnp.int32)

    n_max = (TOP_K * m) // tile + n_experts
    rows = n_max * tile
    tiles = (cnt + (tile - 1)) // tile
    ends = jnp.cumsum(tiles)
    n_tiles = ends[-1]
    start = (ends - tiles) * tile
    used = n_tiles * tile
    plan = jnp.stack([jnp.append(start, used),
                      jnp.append(start + cnt, used),
                      jnp.append(start + tiles * tile, rows)]).astype(jnp.int32)
    idx = jnp.minimum(jnp.arange(n_max, dtype=jnp.int32), n_tiles - 1)
    tile_expert = jnp.searchsorted(ends, idx, side="right").astype(jnp.int32)

    xs = _moe_dispatch(h2, post, plan, rows, tm)
    ys = _moe_experts(xs, w_gu, w_d, tile_expert, n_tiles.reshape(1).astype(jnp.int32), tile)
    return _moe_combine_ln(h2, post, plan, meta, ys, g, bvec, alpha, tm)


def kernel(x, even_w_in, even_pool_w, even_pool_scale, even_w_out, even_ln1_g, even_ln1_b, even_ffn_w_gu, even_ffn_w_down, even_ln2_g, even_ln2_b, odd_w_in, odd_sgu_ln_g, odd_sgu_ln_b, odd_sgu_w, odd_sgu_b, odd_w_out, odd_ln1_g, odd_ln1_b, odd_router, odd_moe_w_gu, odd_moe_w_down, odd_ln2_g, odd_ln2_b):
    b, s, d = x.shape
    depth = even_w_in.shape[0] + odd_w_in.shape[0]
    alpha = (2.0 * depth) ** 0.25
    pool_width = even_pool_scale.shape[1]
    sb_width = (even_w_in.shape[2] - pool_width) // 3
    m = b * s
    tm = min(512, m)
    moe_tile = 256

    h = x.reshape(m, d)
    for layer in range(depth):
        i = layer // 2
        if layer % 2 == 0:
            p, qkv = _in_proj_even(h, even_w_in[i].astype(BF16), pool_width, tm)
            pooled = _pool_mixer(p.reshape(b, s, pool_width), even_pool_w[i].astype(BF16),
                                 even_pool_scale[i], min(512, s))
            attn = _sb_attention(qkv.reshape(b, s, 3 * sb_width), sb_width)
            h = _out_proj_ln(pooled.reshape(m, pool_width), attn.reshape(m, sb_width), h,
                             even_w_out[i].astype(BF16), even_ln1_g[i], even_ln1_b[i], alpha, tm)
            h = _ffn_ln(h, even_ffn_w_gu[i].astype(BF16), even_ffn_w_down[i].astype(BF16),
                        even_ln2_g[i], even_ln2_b[i], alpha, tm, 512)
        else:
            u, v = _in_proj_odd(h, odd_w_in[i].astype(BF16), odd_sgu_ln_g[i], odd_sgu_ln_b[i], tm)
            h = _sgu_out_ln(u, v, odd_sgu_w[i], odd_sgu_b[i], h, odd_w_out[i].astype(BF16),
                            odd_ln1_g[i], odd_ln1_b[i], alpha, tm)
            h = _moe_ln(h, odd_router[i], odd_moe_w_gu[i].astype(BF16), odd_moe_w_down[i].astype(BF16),
                        odd_ln2_g[i], odd_ln2_b[i], alpha, tm, min(moe_tile, m))
    return h.reshape(b, s, d)
```

```python
import functools
import math

import jax
import jax.numpy as jnp
from jax import lax
from jax.experimental import pallas as pl
from jax.experimental.pallas import tpu as pltpu

F32 = jnp.float32
BF16 = jnp.bfloat16

LN_EPS = 1e-5
POOL_WINDOWS = (2, 4, 8, 16)
SB_HEAD_DIM = 128
SGU_GROUPS = 8
SGU_CHUNK = 128
TOP_K = 2

V7X_LANES = 128
V7X_VMEM_BYTES = 64 * 1024 * 1024
BLOCK = 128

F32_EXP_UNDERFLOW = -104.0
NO_WEIGHT = -1e30


def _vmem_limit(block_bytes):
    return int(min(block_bytes * 1.3 + (6 << 20), V7X_VMEM_BYTES - (6 << 20)))


def _layer_norm(r, g, b):
    mu = jnp.mean(r, axis=-1, keepdims=True)
    xc = r - mu
    var = jnp.mean(xc * xc, axis=-1, keepdims=True)
    return xc * lax.rsqrt(var + LN_EPS) * g + b


def _silu(x):
    return x * jax.nn.sigmoid(x)


def _gelu_tanh(x):
    c = math.sqrt(2.0 / math.pi)
    return 0.5 * x * (1.0 + jnp.tanh(c * (x + 0.044715 * (x * x * x))))


def _split_bf16(x):
    hi = x.astype(BF16)
    lo = (x - hi.astype(F32)).astype(BF16)
    return hi, lo


def _row_parts(rows, parts=2):
    step = rows // parts
    return [slice(k * step, (k + 1) * step) for k in range(parts)]


def _const_spec(shape):
    nd = len(shape)
    return pl.BlockSpec(shape, lambda *_: (0,) * nd)


def _pad_to(x, axis, mult):
    pad = (-x.shape[axis]) % mult
    if pad == 0:
        return x
    widths = [(0, 0)] * x.ndim
    widths[axis] = (0, pad)
    return jnp.pad(x, widths)


def _in_proj_even_kernel(x_ref, w_ref, p_ref, qkv_ref, *, pool_width):
    xb = x_ref[...].astype(BF16)
    p_ref[...] = jnp.dot(xb, w_ref[:, :pool_width], preferred_element_type=F32)
    n_rest = (w_ref.shape[1] - pool_width) // pool_width
    for c in range(n_rest):
        lo = pool_width * (c + 1)
        qkv_ref[:, c * pool_width:(c + 1) * pool_width] = jnp.dot(
            xb, w_ref[:, lo:lo + pool_width], preferred_element_type=F32).astype(BF16)


def _in_proj_even(x2, w_in, pool_width, tm):
    m, d = x2.shape
    n = w_in.shape[1]
    blk = 2 * tm * d * 4 + 2 * d * n * 2 + 2 * tm * pool_width * 4 + 2 * tm * (n - pool_width) * 2 + tm * n * 4
    return pl.pallas_call(
        functools.partial(_in_proj_even_kernel, pool_width=pool_width),
        grid=(m // tm,),
        in_specs=[pl.BlockSpec((tm, d), lambda i: (i, 0)), _const_spec((d, n))],
        out_specs=[pl.BlockSpec((tm, pool_width), lambda i: (i, 0)),
                   pl.BlockSpec((tm, n - pool_width), lambda i: (i, 0))],
        out_shape=[jax.ShapeDtypeStruct((m, pool_width), F32),
                   jax.ShapeDtypeStruct((m, n - pool_width), BF16)],
        compiler_params=pltpu.CompilerParams(
            dimension_semantics=("arbitrary",), vmem_limit_bytes=_vmem_limit(blk)),
        name="in_proj_even",
    )(x2, w_in)


def _pool_kernel(pc_ref, pp_ref, w_ref, sc_ref, o_ref, hi_ref, lo_ref, *, ts, group_dim):
    i = pl.program_id(1)
    prev = jnp.where(i > 0, pp_ref[0], 0.0)
    h, l = _split_bf16(prev)
    hi_ref[0:BLOCK, :] = h
    lo_ref[0:BLOCK, :] = l
    h, l = _split_bf16(pc_ref[0])
    hi_ref[BLOCK:, :] = h
    lo_ref[BLOCK:, :] = l

    ii = lax.broadcasted_iota(jnp.int32, (BLOCK, 2 * BLOCK), 0) + BLOCK
    jj = lax.broadcasted_iota(jnp.int32, (BLOCK, 2 * BLOCK), 1)
    row = lax.broadcasted_iota(jnp.int32, (BLOCK, 1), 0)
    for g, w in enumerate(POOL_WINDOWS):
        band = jnp.where((jj <= ii) & (jj > ii - w), 1.0, 0.0).astype(BF16)
        cs = slice(g * group_dim, (g + 1) * group_dim)
        for r in range(ts // BLOCK):
            rs = slice(r * BLOCK, r * BLOCK + 2 * BLOCK)
            wsum = (jnp.dot(band, hi_ref[rs, cs], preferred_element_type=F32)
                    + jnp.dot(band, lo_ref[rs, cs], preferred_element_type=F32))
            t = i * ts + r * BLOCK + row
            count = jnp.minimum(t + 1, w).astype(F32)
            d = wsum / count - pc_ref[0, r * BLOCK:(r + 1) * BLOCK, cs]
            y = jnp.dot(d.astype(BF16), w_ref[g], preferred_element_type=F32) * sc_ref[:, cs]
            o_ref[0, r * BLOCK:(r + 1) * BLOCK, cs] = y.astype(o_ref.dtype)


def _pool_mixer(p, pool_w, pool_scale, ts):
    b, s, c = p.shape
    groups = len(POOL_WINDOWS)
    gd = c // groups
    per = ts // BLOCK
    blk = 2 * ts * c * 4 + 2 * BLOCK * c * 4 + 2 * groups * gd * gd * 2 + 2 * ts * c * 2 + 2 * (ts + BLOCK) * c * 2
    return pl.pallas_call(
        functools.partial(_pool_kernel, ts=ts, group_dim=gd),
        grid=(b, s // ts),
        in_specs=[pl.BlockSpec((1, ts, c), lambda bi, i: (bi, i, 0)),
                  pl.BlockSpec((1, BLOCK, c), lambda bi, i: (bi, jnp.maximum(i * per - 1, 0), 0)),
                  _const_spec((groups, gd, gd)),
                  _const_spec((1, c))],
        out_specs=pl.BlockSpec((1, ts, c), lambda bi, i: (bi, i, 0)),
        out_shape=jax.ShapeDtypeStruct((b, s, c), BF16),
        scratch_shapes=[pltpu.VMEM((ts + BLOCK, c), BF16), pltpu.VMEM((ts + BLOCK, c), BF16)],
        compiler_params=pltpu.CompilerParams(
            dimension_semantics=("arbitrary", "arbitrary"), vmem_limit_bytes=_vmem_limit(blk)),
        name="pool_mixer",
    )(p, p, pool_w, pool_scale.reshape(1, c))


def _sb_attn_kernel(q_ref, k_ref, v_ref, o_ref, *scratch, n_blocks, group, scale):
    acc_refs, carry_refs = scratch[:group], scratch[group:]
    row = lax.broadcasted_iota(jnp.int32, (BLOCK, BLOCK), 0)
    col = lax.broadcasted_iota(jnp.int32, (BLOCK, BLOCK), 1)
    causal = col < row
    jj = lax.broadcasted_iota(jnp.int32, (BLOCK, 2 * BLOCK), 0)
    ss = lax.broadcasted_iota(jnp.int32, (BLOCK, 2 * BLOCK), 1)
    tmat = jnp.where((jj > ss) | (ss >= BLOCK), 1.0, 0.0).astype(BF16)
    tmat2 = jnp.concatenate([tmat, tmat], axis=0)

    def tiles(qs, ks, vs, carries, mask):
        n = len(qs)
        log_betas, parts = [], []
        for q, kj in zip(qs, ks):
            z = lax.dot_general(q, kj, (((1,), (1,)), ((), ())), preferred_element_type=F32) * scale
            sp = jnp.maximum(z, 0.0) + jnp.log(1.0 + jnp.exp(-jnp.abs(z)))
            log_betas.append(z - sp)
            if mask is not None:
                sp = jnp.where(mask, sp, 0.0)
            parts.append(jnp.concatenate(_split_bf16(sp), axis=1))
        sums = jnp.dot(jnp.concatenate(parts, axis=0), tmat2, preferred_element_type=F32)
        pvs, new_carries = [], []
        for r in range(n):
            s = sums[r * BLOCK:(r + 1) * BLOCK]
            a = jnp.exp(log_betas[r] - s[:, :BLOCK] + carries[r])
            if mask is not None:
                a = jnp.where(mask, a, 0.0)
            pvs.append(jnp.dot(a.astype(BF16), vs[r], preferred_element_type=F32))
            new_carries.append(carries[r] - s[:, BLOCK:])
        return pvs, new_carries

    def top_of(carries):
        m = carries[0]
        for c in carries[1:]:
            m = jnp.maximum(m, c)
        return jnp.max(m)

    def q_group(gi, _):
        base = gi * group
        offs = [pl.multiple_of((base + r) * BLOCK, BLOCK) for r in range(group)]
        load = lambda ref, off: ref[0, pl.ds(off, BLOCK), :]
        pvs, carries = tiles([load(q_ref, o) for o in offs], [load(k_ref, o) for o in offs],
                             [load(v_ref, o) for o in offs], [jnp.zeros((BLOCK, BLOCK), F32)] * group, causal)
        for r in range(group):
            acc_refs[r][...] = pvs[r]
            carry_refs[r][...] = carries[r]

        def cond(c):
            k, top = c
            return jnp.logical_and(k < base + group, top > F32_EXP_UNDERFLOW)

        def body(c):
            k, _ = c
            koffs, carries = [], []
            for r in range(group):
                j = base + r - k
                carries.append(jnp.where(j >= 0, carry_refs[r][...], NO_WEIGHT))
                koffs.append(pl.multiple_of(jnp.maximum(j, 0) * BLOCK, BLOCK))
            pvs, carries = tiles([load(q_ref, o) for o in offs], [load(k_ref, o) for o in koffs],
                                 [load(v_ref, o) for o in koffs], carries, None)
            for r in range(group):
                acc_refs[r][...] += pvs[r]
                carry_refs[r][...] = carries[r]
            return k + 1, top_of(carries)

        lax.while_loop(cond, body, (jnp.int32(1), top_of(carries)))
        for r in range(group):
            o_ref[0, pl.ds(offs[r], BLOCK), :] = acc_refs[r][...].astype(o_ref.dtype)
        return 0

    lax.fori_loop(0, n_blocks // group, q_group, 0)


def _sb_attention(qkv, width):
    b, s, _ = qkv.shape
    heads = width // SB_HEAD_DIM
    n_blocks = s // BLOCK
    group = math.gcd(n_blocks, 8)
    blk = 2 * 4 * s * SB_HEAD_DIM * 2 + 2 * group * BLOCK * BLOCK * 4
    spec = lambda off: pl.BlockSpec((1, s, SB_HEAD_DIM), lambda bi, h: (bi, 0, off * heads + h))
    return pl.pallas_call(
        functools.partial(_sb_attn_kernel, n_blocks=n_blocks, group=group, scale=1.0 / math.sqrt(SB_HEAD_DIM)),
        grid=(b, heads),
        in_specs=[spec(0), spec(1), spec(2)],
        out_specs=pl.BlockSpec((1, s, SB_HEAD_DIM), lambda bi, h: (bi, 0, h)),
        out_shape=jax.ShapeDtypeStruct((b, s, width), BF16),
        scratch_shapes=[pltpu.VMEM((BLOCK, BLOCK), F32)] * (2 * group),
        compiler_params=pltpu.CompilerParams(
            dimension_semantics=("arbitrary", "arbitrary"), vmem_limit_bytes=_vmem_limit(blk)),
        name="sb_attention",
    )(qkv, qkv, qkv)


def _out_proj_ln_kernel(a1_ref, a2_ref, x_ref, w_ref, g_ref, b_ref, o_ref, *, alpha):
    k1 = a1_ref.shape[1]
    for rs in _row_parts(x_ref.shape[0]):
        y = (jnp.dot(a1_ref[rs, :], w_ref[:k1, :], preferred_element_type=F32)
             + jnp.dot(a2_ref[rs, :], w_ref[k1:, :], preferred_element_type=F32))
        o_ref[rs, :] = _layer_norm(alpha * x_ref[rs, :] + y, g_ref[...], b_ref[...])


def _out_proj_ln(a1, a2, x2, w_out, g, bvec, alpha, tm):
    m, d = x2.shape
    k1, k2 = a1.shape[1], a2.shape[1]
    blk = 2 * tm * (k1 + k2) * 2 + 2 * tm * d * 4 + 2 * (k1 + k2) * d * 2 + 2 * tm * d * 4 + tm * d * 4
    return pl.pallas_call(
        functools.partial(_out_proj_ln_kernel, alpha=alpha),
        grid=(m // tm,),
        in_specs=[pl.BlockSpec((tm, k1), lambda i: (i, 0)),
                  pl.BlockSpec((tm, k2), lambda i: (i, 0)),
                  pl.BlockSpec((tm, d), lambda i: (i, 0)),
                  _const_spec((k1 + k2, d)), _const_spec((1, d)), _const_spec((1, d))],
        out_specs=pl.BlockSpec((tm, d), lambda i: (i, 0)),
        out_shape=jax.ShapeDtypeStruct((m, d), F32),
        compiler_params=pltpu.CompilerParams(
            dimension_semantics=("arbitrary",), vmem_limit_bytes=_vmem_limit(blk)),
        name="out_proj_ln_even",
    )(a1, a2, x2, w_out, g.reshape(1, d), bvec.reshape(1, d))


def _ffn_kernel(h_ref, wg_ref, wu_ref, wd_ref, g_ref, b_ref, o_ref, hb_ref, acc_ref, *, alpha):
    j = pl.program_id(1)

    @pl.when(j == 0)
    def _():
        hb_ref[...] = h_ref[...].astype(BF16)
        acc_ref[...] = jnp.zeros_like(acc_ref)

    hb = hb_ref[...]
    gate = jnp.dot(hb, wg_ref[...], preferred_element_type=F32)
    up = jnp.dot(hb, wu_ref[...], preferred_element_type=F32)
    act = (_silu(gate) * up).astype(BF16)
    acc_ref[...] += jnp.dot(act, wd_ref[...], preferred_element_type=F32)

    @pl.when(j == pl.num_programs(1) - 1)
    def _():
        o_ref[...] = _layer_norm(alpha * h_ref[...] + acc_ref[...], g_ref[...], b_ref[...])


def _ffn_ln(h2, w_gu, w_down, g, bvec, alpha, tm, tf):
    m, d = h2.shape
    f = w_down.shape[0]
    nf = f // tf
    blk = 2 * tm * d * 4 + 2 * 2 * d * tf * 2 + 2 * tf * d * 2 + 2 * tm * d * 4 + tm * d * 2 + tm * d * 4 + 3 * tm * tf * 4
    return pl.pallas_call(
        functools.partial(_ffn_kernel, alpha=alpha),
        grid=(m // tm, nf),
        in_specs=[pl.BlockSpec((tm, d), lambda i, j: (i, 0)),
                  pl.BlockSpec((d, tf), lambda i, j: (0, j)),
                  pl.BlockSpec((d, tf), lambda i, j: (0, j + nf)),
                  pl.BlockSpec((tf, d), lambda i, j: (j, 0)),
                  _const_spec((1, d)), _const_spec((1, d))],
        out_specs=pl.BlockSpec((tm, d), lambda i, j: (i, 0)),
        out_shape=jax.ShapeDtypeStruct((m, d), F32),
        scratch_shapes=[pltpu.VMEM((tm, d), BF16), pltpu.VMEM((tm, d), F32)],
        compiler_params=pltpu.CompilerParams(
            dimension_semantics=("arbitrary", "arbitrary"), vmem_limit_bytes=_vmem_limit(blk)),
        name="ffn_ln_even",
    )(h2, w_gu, w_gu, w_down, g.reshape(1, d), bvec.reshape(1, d))


def _in_proj_odd_kernel(h_ref, w_ref, g_ref, b_ref, u_ref, v_ref):
    hb = h_ref[...].astype(BF16)
    width = u_ref.shape[1]
    v = _gelu_tanh(jnp.dot(hb, w_ref[:, width:], preferred_element_type=F32))
    v_ref[...] = _layer_norm(v, g_ref[...], b_ref[...]).astype(BF16)
    for cs in _row_parts(width):
        u_ref[:, cs] = _gelu_tanh(jnp.dot(hb, w_ref[:, cs], preferred_element_type=F32))


def _in_proj_odd(h2, w_in, g, bvec, tm):
    m, d = h2.shape
    n = w_in.shape[1]
    width = n // 2
    blk = 2 * tm * d * 4 + 2 * d * n * 2 + 2 * tm * width * 4 + 2 * tm * width * 2 + 2 * tm * width * 4
    return pl.pallas_call(
        _in_proj_odd_kernel,
        grid=(m // tm,),
        in_specs=[pl.BlockSpec((tm, d), lambda i: (i, 0)), _const_spec((d, n)),
                  _const_spec((1, width)), _const_spec((1, width))],
        out_specs=[pl.BlockSpec((tm, width), lambda i: (i, 0)),
                   pl.BlockSpec((tm, width), lambda i: (i, 0))],
        out_shape=[jax.ShapeDtypeStruct((m, width), F32), jax.ShapeDtypeStruct((m, width), BF16)],
        compiler_params=pltpu.CompilerParams(
            dimension_semantics=("arbitrary",), vmem_limit_bytes=_vmem_limit(blk)),
        name="in_proj_odd",
    )(h2, w_in, g.reshape(1, width), bvec.reshape(1, width))


def _sgu_out_ln_kernel(u_ref, v_ref, ws_ref, bs_ref, h_ref, w_ref, g_ref, b_ref, o_ref, gated_ref, *, alpha):
    tm, width = u_ref.shape
    gd = width // SGU_GROUPS
    row = lax.broadcasted_iota(jnp.int32, (SGU_CHUNK, SGU_CHUNK), 0)
    col = lax.broadcasted_iota(jnp.int32, (SGU_CHUNK, SGU_CHUNK), 1)
    lower = col <= row
    for gi in range(SGU_GROUPS):
        ws = jnp.where(lower, ws_ref[gi], 0.0).astype(BF16)
        bias = bs_ref[gi]
        cs = slice(gi * gd, (gi + 1) * gd)
        for n in range(tm // SGU_CHUNK):
            rs = slice(n * SGU_CHUNK, (n + 1) * SGU_CHUNK)
            mixed = jnp.dot(ws, v_ref[rs, cs], preferred_element_type=F32) + bias
            gated_ref[rs, cs] = (u_ref[rs, cs] * mixed).astype(BF16)
    for rs in _row_parts(tm):
        y = jnp.dot(gated_ref[rs, :], w_ref[...], preferred_element_type=F32)
        o_ref[rs, :] = _layer_norm(alpha * h_ref[rs, :] + y, g_ref[...], b_ref[...])


def _sgu_out_ln(u, v, w_s, b_s, h2, w_out, g, bvec, alpha, tm):
    m, d = h2.shape
    width = u.shape[1]
    blk = (2 * tm * width * 4 + 2 * tm * width * 2 + 2 * tm * d * 4 + 2 * width * d * 2 + 2 * tm * d * 4
           + tm * width * 2 + tm * d * 4 + 4 * SGU_GROUPS * SGU_CHUNK * V7X_LANES * 4)
    return pl.pallas_call(
        functools.partial(_sgu_out_ln_kernel, alpha=alpha),
        grid=(m // tm,),
        in_specs=[pl.BlockSpec((tm, width), lambda i: (i, 0)),
                  pl.BlockSpec((tm, width), lambda i: (i, 0)),
                  _const_spec((SGU_GROUPS, SGU_CHUNK, SGU_CHUNK)),
                  _const_spec((SGU_GROUPS, SGU_CHUNK, 1)),
                  pl.BlockSpec((tm, d), lambda i: (i, 0)),
                  _const_spec((width, d)), _const_spec((1, d)), _const_spec((1, d))],
        out_specs=pl.BlockSpec((tm, d), lambda i: (i, 0)),
        out_shape=jax.ShapeDtypeStruct((m, d), F32),
        scratch_shapes=[pltpu.VMEM((tm, width), BF16)],
        compiler_params=pltpu.CompilerParams(
            dimension_semantics=("arbitrary",), vmem_limit_bytes=_vmem_limit(blk)),
        name="sgu_out_ln_odd",
    )(u, v, w_s, b_s.reshape(SGU_GROUPS, SGU_CHUNK, 1), h2, w_out, g.reshape(1, d), bvec.reshape(1, d))


META_W1, META_W2, META_D1, META_D2 = range(4)
TAB_SIZE, TAB_OFF, TAB_RUN = 0, 8, 16
TAB_DST = TAB_RUN
GROUP_ALIGN = 16
MAX_EXPERTS = 8


def _grouped_rows(tm):
    return -(-(TOP_K * tm + MAX_EXPERTS * (GROUP_ALIGN - 1)) // V7X_LANES) * V7X_LANES


def _route_kernel(h_ref, wr_ref, meta_ref, post_ref, tab_ref, run_ref, *, n_experts):
    @pl.when(pl.program_id(0) == 0)
    def _():
        run_ref[...] = jnp.zeros_like(run_ref)

    tm = h_ref.shape[0]
    h_hi, h_lo = _split_bf16(h_ref[...])
    w_hi, w_lo = _split_bf16(wr_ref[...])
    logits = (jnp.dot(h_hi, w_hi, preferred_element_type=F32) + jnp.dot(h_lo, w_hi, preferred_element_type=F32)
              + jnp.dot(h_hi, w_lo, preferred_element_type=F32))
    lanes = logits.shape[1]
    lane = lax.broadcasted_iota(jnp.int32, logits.shape, 1)
    neg = jnp.float32(-jnp.inf)
    lg = jnp.where(lane < n_experts, logits, neg)
    m1 = jnp.max(lg, axis=1, keepdims=True)
    i1 = jnp.min(jnp.where(lg == m1, lane, lanes), axis=1, keepdims=True)
    lg2 = jnp.where(lane == i1, neg, lg)
    m2 = jnp.max(lg2, axis=1, keepdims=True)
    i2 = jnp.min(jnp.where(lg2 == m2, lane, lanes), axis=1, keepdims=True)
    e2 = jnp.exp(m2 - m1)
    denom = 1.0 + e2
    w1 = 1.0 / denom
    w2 = e2 / denom

    onehot = jnp.where((lane == i1) | (lane == i2), 1.0, 0.0)
    row = lax.broadcasted_iota(jnp.int32, (tm, tm), 0)
    col = lax.broadcasted_iota(jnp.int32, (tm, tm), 1)
    earlier = jnp.where(col < row, 1.0, 0.0).astype(BF16)
    within = jnp.dot(earlier, onehot.astype(BF16), preferred_element_type=F32)
    size = jnp.sum(onehot, axis=0, keepdims=True)
    size = jnp.ceil(size * (1.0 / GROUP_ALIGN)) * GROUP_ALIGN
    er = lax.broadcasted_iota(jnp.int32, (lanes, lanes), 0)
    ec = lax.broadcasted_iota(jnp.int32, (lanes, lanes), 1)
    before = jnp.where(er < ec, 1.0, 0.0).astype(BF16)
    size8 = jnp.broadcast_to(size, (8, lanes))
    off8 = jnp.dot(size8.astype(BF16), before, preferred_element_type=F32)
    where_in_tile = within + off8[0:1, :]
    d1 = jnp.sum(jnp.where(lane == i1, where_in_tile, 0.0), axis=1, keepdims=True)
    d2 = jnp.sum(jnp.where(lane == i2, where_in_tile, 0.0), axis=1, keepdims=True)

    meta = jnp.zeros_like(logits)
    for k, val in {META_W1: w1, META_W2: w2, META_D1: d1, META_D2: d2}.items():
        meta = jnp.where(lane == k, val, meta)
    meta_ref[...] = meta
    post_ref[...] = meta.T[:post_ref.shape[0], :].astype(jnp.int32)

    run8 = jnp.broadcast_to(run_ref[...], (8, lanes))
    tab = jnp.where(lane[:8] < TAB_OFF, size8, 0.0)
    tab = tab + pltpu.roll(jnp.where(lane[:8] < n_experts, off8, 0.0), TAB_OFF, 1)
    tab = tab + pltpu.roll(jnp.where(lane[:8] < n_experts, run8, 0.0), TAB_RUN, 1)
    tab_ref[0] = tab
    run_ref[...] += size


def _moe_route(h2, w_router, tm):
    m, d = h2.shape
    n_experts = w_router.shape[1]
    assert n_experts <= MAX_EXPERTS
    w_r = _pad_to(w_router, 1, V7X_LANES)
    lanes = w_r.shape[1]
    blk = 2 * tm * d * 4 + 2 * d * lanes * 4 + 4 * tm * lanes * 4 + tm * tm * 6 + 8 * tm * lanes * 4 + tm * d * 4
    return pl.pallas_call(
        functools.partial(_route_kernel, n_experts=n_experts),
        grid=(m // tm,),
        in_specs=[pl.BlockSpec((tm, d), lambda i: (i, 0)), _const_spec((d, lanes))],
        out_specs=[pl.BlockSpec((tm, lanes), lambda i: (i, 0)),
                   pl.BlockSpec((8, tm), lambda i: (0, i)),
                   pl.BlockSpec((1, 8, lanes), lambda i: (i, 0, 0))],
        out_shape=[jax.ShapeDtypeStruct((m, lanes), F32),
                   jax.ShapeDtypeStruct((8, m), jnp.int32),
                   jax.ShapeDtypeStruct((m // tm, 8, lanes), F32)],
        scratch_shapes=[pltpu.VMEM((1, lanes), F32)],
        compiler_params=pltpu.CompilerParams(
            dimension_semantics=("arbitrary",), vmem_limit_bytes=_vmem_limit(blk)),
        name="moe_route",
    )(h2, w_r)


def _group_copy(src_ref, src_row, dst_ref, dst_row, sem):
    return pltpu.make_async_copy(src_ref.at[pl.ds(src_row, GROUP_ALIGN), :],
                                 dst_ref.at[pl.ds(dst_row, GROUP_ALIGN), :], sem)


def _for_each_group_slab(tab_ref, i, n_experts, fn):
    for e in range(n_experts):
        size, off, dst = tab_ref[i, TAB_SIZE + e], tab_ref[i, TAB_OFF + e], tab_ref[i, TAB_DST + e]

        def slab(k, _):
            step = pl.multiple_of(k * GROUP_ALIGN, GROUP_ALIGN)
            fn(pl.multiple_of(off + step, GROUP_ALIGN), pl.multiple_of(dst + step, GROUP_ALIGN))
            return 0

        lax.fori_loop(0, size // GROUP_ALIGN, slab, 0)


def _dispatch_kernel(tab_ref, fill_ref, post_ref, h_ref, xs_ref, grouped_ref, zero_ref, sem, *, n_experts):
    i = pl.program_id(0)
    rows = grouped_ref.shape[0]
    tm = h_ref.shape[0]
    slot = lax.broadcasted_iota(jnp.int32, (rows, tm), 0)
    d1 = post_ref[META_D1:META_D1 + 1, :]
    d2 = post_ref[META_D2:META_D2 + 1, :]
    select = jnp.where((slot == d1) | (slot == d2), 1.0, 0.0).astype(BF16)
    grouped_ref[...] = jnp.dot(select, h_ref[...].astype(BF16), preferred_element_type=F32).astype(BF16)

    _for_each_group_slab(tab_ref, i, n_experts,
                         lambda src, dst: _group_copy(grouped_ref, src, xs_ref, dst, sem).start())
    _for_each_group_slab(tab_ref, i, n_experts,
                         lambda src, dst: _group_copy(grouped_ref, src, xs_ref, dst, sem).wait())

    @pl.when(i == pl.num_programs(0) - 1)
    def _():
        zero_ref[...] = jnp.zeros_like(zero_ref)
        for e in range(fill_ref.shape[1]):
            begin = fill_ref[0, e]
            n = (fill_ref[1, e] - begin) // GROUP_ALIGN
            fill = lambda k: _group_copy(zero_ref, 0, xs_ref, pl.multiple_of(begin + k * GROUP_ALIGN, GROUP_ALIGN), sem)

            def fill_start(k, _):
                fill(k).start()
                return 0

            def fill_wait(k, _):
                fill(k).wait()
                return 0

            lax.fori_loop(0, n, fill_start, 0)
            lax.fori_loop(0, n, fill_wait, 0)


def _moe_dispatch(h2, post, tab, fill, rows, n_experts, tm):
    m, d = h2.shape
    grouped = _grouped_rows(tm)
    blk = 2 * tm * d * 4 + 2 * 8 * tm * 4 + grouped * d * 2 + grouped * d * 4 + grouped * tm * 4 + tm * d * 2
    return pl.pallas_call(
        functools.partial(_dispatch_kernel, n_experts=n_experts),
        grid=(m // tm,),
        in_specs=[pl.BlockSpec(memory_space=pltpu.SMEM),
                  pl.BlockSpec(memory_space=pltpu.SMEM),
                  pl.BlockSpec((8, tm), lambda i: (0, i)),
                  pl.BlockSpec((tm, d), lambda i: (i, 0))],
        out_specs=pl.BlockSpec(memory_space=pl.ANY),
        out_shape=jax.ShapeDtypeStruct((rows, d), BF16),
        scratch_shapes=[pltpu.VMEM((grouped, d), BF16), pltpu.VMEM((GROUP_ALIGN, d), BF16),
                        pltpu.SemaphoreType.DMA(())],
        compiler_params=pltpu.CompilerParams(
            dimension_semantics=("arbitrary",), vmem_limit_bytes=_vmem_limit(blk)),
        name="moe_dispatch",
    )(tab, fill, post, h2)


def _experts_kernel(te_ref, nt_ref, xs_ref, wgu_ref, wd_ref, ys_ref):
    occupied = pl.program_id(0) < nt_ref[0]

    @pl.when(occupied)
    def _():
        fe = wd_ref.shape[1]
        gu = jnp.dot(xs_ref[...], wgu_ref[0], preferred_element_type=F32)
        act = (_silu(gu[:, :fe]) * gu[:, fe:]).astype(BF16)
        ys_ref[...] = jnp.dot(act, wd_ref[0], preferred_element_type=F32).astype(ys_ref.dtype)

    @pl.when(jnp.logical_not(occupied))
    def _():
        ys_ref[...] = jnp.zeros_like(ys_ref)


def _moe_experts(xs, w_gu, w_d, tile_expert, n_tiles, tile):
    rows, d = xs.shape
    _, _, fe2 = w_gu.shape
    fe = w_d.shape[1]
    blk = 2 * tile * d * 2 * 2 + 2 * d * fe2 * 2 + 2 * fe * d * 2 + tile * fe2 * 4 + tile * fe * 6 + tile * d * 4
    grid_spec = pltpu.PrefetchScalarGridSpec(
        num_scalar_prefetch=2,
        grid=(rows // tile,),
        in_specs=[pl.BlockSpec((tile, d), lambda i, te, nt: (i, 0)),
                  pl.BlockSpec((1, d, fe2), lambda i, te, nt: (te[i], 0, 0)),
                  pl.BlockSpec((1, fe, d), lambda i, te, nt: (te[i], 0, 0))],
        out_specs=pl.BlockSpec((tile, d), lambda i, te, nt: (i, 0)),
    )
    return pl.pallas_call(
        _experts_kernel,
        grid_spec=grid_spec,
        out_shape=jax.ShapeDtypeStruct((rows, d), BF16),
        compiler_params=pltpu.CompilerParams(
            dimension_semantics=("arbitrary",), vmem_limit_bytes=_vmem_limit(blk)),
        name="moe_experts",
    )(tile_expert, n_tiles, xs, w_gu, w_d)


def _combine_ln_kernel(tab_ref, h_ref, meta_ref, ys_ref, g_ref, b_ref, o_ref, grouped_ref, sem, *, alpha, n_experts):
    i = pl.program_id(0)

    @pl.when(i == 0)
    def _():
        grouped_ref[...] = jnp.zeros_like(grouped_ref)

    _for_each_group_slab(tab_ref, i, n_experts,
                         lambda row, src: _group_copy(ys_ref, src, grouped_ref, row, sem).start())
    _for_each_group_slab(tab_ref, i, n_experts,
                         lambda row, src: _group_copy(ys_ref, src, grouped_ref, row, sem).wait())

    tm = h_ref.shape[0]
    rows = grouped_ref.shape[0]
    meta = meta_ref[...]
    slot = lax.broadcasted_iota(jnp.int32, (tm, rows), 1)
    grouped = grouped_ref[...]
    moe = jnp.zeros((tm, grouped.shape[1]), F32)
    for w_lane, d_lane in ((META_W1, META_D1), (META_W2, META_D2)):
        pick = jnp.where(slot == meta[:, d_lane:d_lane + 1].astype(jnp.int32), 1.0, 0.0).astype(BF16)
        moe = moe + meta[:, w_lane:w_lane + 1] * jnp.dot(pick, grouped, preferred_element_type=F32)
    o_ref[...] = _layer_norm(alpha * h_ref[...] + moe, g_ref[...], b_ref[...])


def _moe_combine_ln(h2, tab, meta, ys, g, bvec, alpha, n_experts, tm):
    m, d = h2.shape
    lanes = meta.shape[1]
    grouped = _grouped_rows(tm)
    blk = 2 * tm * d * 4 * 2 + 2 * tm * lanes * 4 + grouped * d * 2 + 2 * tm * grouped * 2 + 3 * tm * d * 4
    return pl.pallas_call(
        functools.partial(_combine_ln_kernel, alpha=alpha, n_experts=n_experts),
        grid=(m // tm,),
        in_specs=[pl.BlockSpec(memory_space=pltpu.SMEM),
                  pl.BlockSpec((tm, d), lambda i: (i, 0)),
                  pl.BlockSpec((tm, lanes), lambda i: (i, 0)),
                  pl.BlockSpec(memory_space=pl.ANY),
                  _const_spec((1, d)), _const_spec((1, d))],
        out_specs=pl.BlockSpec((tm, d), lambda i: (i, 0)),
        out_shape=jax.ShapeDtypeStruct((m, d), F32),
        scratch_shapes=[pltpu.VMEM((grouped, d), BF16), pltpu.SemaphoreType.DMA(())],
        compiler_params=pltpu.CompilerParams(
            dimension_semantics=("arbitrary",), vmem_limit_bytes=_vmem_limit(blk)),
        name="moe_combine_ln",
    )(tab, h2, meta, ys, g.reshape(1, d), bvec.reshape(1, d))


def _moe_ln(h2, w_router, w_gu, w_d, g, bvec, alpha, tm, tile):
    m, d = h2.shape
    n_experts = w_router.shape[1]
    n_tok_tiles = m // tm
    meta, post, tab = _moe_route(h2, w_router, tm)

    tab = tab[:, 0, :].astype(jnp.int32)
    sizes = tab[:, TAB_SIZE:TAB_SIZE + n_experts]
    region = jnp.sum(sizes, axis=0)
    n_max = -(-(TOP_K * m + n_tok_tiles * n_experts * (GROUP_ALIGN - 1)) // tile) + n_experts
    rows = n_max * tile
    tiles = (region + (tile - 1)) // tile
    ends = jnp.cumsum(tiles)
    n_tiles = ends[-1]
    start = (ends - tiles) * tile
    tab = tab.at[:, TAB_DST:TAB_DST + n_experts].add(start[None, :])
    fill = jnp.stack([jnp.append(start + region, n_tiles * tile),
                      jnp.append(start + tiles * tile, rows)]).astype(jnp.int32)
    idx = jnp.minimum(jnp.arange(n_max, dtype=jnp.int32), n_tiles - 1)
    tile_expert = jnp.sum(idx[:, None] >= ends[None, :], axis=1).astype(jnp.int32)

    xs = _moe_dispatch(h2, post, tab, fill, rows, n_experts, tm)
    ys = _moe_experts(xs, w_gu, w_d, tile_expert, n_tiles.reshape(1).astype(jnp.int32), tile)
    return _moe_combine_ln(h2, tab, meta, ys, g, bvec, alpha, n_experts, tm)


def kernel(x, even_w_in, even_pool_w, even_pool_scale, even_w_out, even_ln1_g, even_ln1_b, even_ffn_w_gu, even_ffn_w_down, even_ln2_g, even_ln2_b, odd_w_in, odd_sgu_ln_g, odd_sgu_ln_b, odd_sgu_w, odd_sgu_b, odd_w_out, odd_ln1_g, odd_ln1_b, odd_router, odd_moe_w_gu, odd_moe_w_down, odd_ln2_g, odd_ln2_b):
    b, s, d = x.shape
    depth = even_w_in.shape[0] + odd_w_in.shape[0]
    alpha = (2.0 * depth) ** 0.25
    pool_width = even_pool_scale.shape[1]
    sb_width = (even_w_in.shape[2] - pool_width) // 3
    m = b * s
    tm = min(512, m)
    moe_tile = 256

    h = x.reshape(m, d)
    for layer in range(depth):
        i = layer // 2
        if layer % 2 == 0:
            p, qkv = _in_proj_even(h, even_w_in[i].astype(BF16), pool_width, tm)
            pooled = _pool_mixer(p.reshape(b, s, pool_width), even_pool_w[i].astype(BF16),
                                 even_pool_scale[i], min(512, s))
            attn = _sb_attention(qkv.reshape(b, s, 3 * sb_width), sb_width)
            h = _out_proj_ln(pooled.reshape(m, pool_width), attn.reshape(m, sb_width), h,
                             even_w_out[i].astype(BF16), even_ln1_g[i], even_ln1_b[i], alpha, tm)
            h = _ffn_ln(h, even_ffn_w_gu[i].astype(BF16), even_ffn_w_down[i].astype(BF16),
                        even_ln2_g[i], even_ln2_b[i], alpha, tm, 512)
        else:
            u, v = _in_proj_odd(h, odd_w_in[i].astype(BF16), odd_sgu_ln_g[i], odd_sgu_ln_b[i], tm)
            h = _sgu_out_ln(u, v, odd_sgu_w[i], odd_sgu_b[i], h, odd_w_out[i].astype(BF16),
                            odd_ln1_g[i], odd_ln1_b[i], alpha, tm)
            h = _moe_ln(h, odd_router[i], odd_moe_w_gu[i].astype(BF16), odd_moe_w_down[i].astype(BF16),
                        odd_ln2_g[i], odd_ln2_b[i], alpha, tm, min(moe_tile, m))
    return h.reshape(b, s, d)
```

```python
import functools
import math

import jax
import jax.numpy as jnp
from jax import lax
from jax.experimental import pallas as pl
from jax.experimental.pallas import tpu as pltpu

F32 = jnp.float32
BF16 = jnp.bfloat16

LN_EPS = 1e-5
POOL_WINDOWS = (2, 4, 8, 16)
SB_HEAD_DIM = 128
SGU_GROUPS = 8
SGU_CHUNK = 128
TOP_K = 2

V7X_LANES = 128
V7X_VMEM_BYTES = 64 * 1024 * 1024
BLOCK = 128

F32_EXP_UNDERFLOW = -104.0
NO_WEIGHT = -1e30


def _vmem_limit(block_bytes):
    return int(min(block_bytes * 1.3 + (6 << 20), V7X_VMEM_BYTES - (3 << 20)))


def _layer_norm(r, g, b):
    mu = jnp.mean(r, axis=-1, keepdims=True)
    xc = r - mu
    var = jnp.mean(xc * xc, axis=-1, keepdims=True)
    return xc * lax.rsqrt(var + LN_EPS) * g + b


def _silu(x):
    return x * jax.nn.sigmoid(x)


def _gelu_tanh(x):
    c = math.sqrt(2.0 / math.pi)
    return 0.5 * x * (1.0 + jnp.tanh(c * (x + 0.044715 * (x * x * x))))


def _split_bf16(x):
    hi = x.astype(BF16)
    lo = (x - hi.astype(F32)).astype(BF16)
    return hi, lo


def _row_parts(rows, parts=2):
    step = rows // parts
    return [slice(k * step, (k + 1) * step) for k in range(parts)]


def _const_spec(shape):
    nd = len(shape)
    return pl.BlockSpec(shape, lambda *_: (0,) * nd)


def _pad_to(x, axis, mult):
    pad = (-x.shape[axis]) % mult
    if pad == 0:
        return x
    widths = [(0, 0)] * x.ndim
    widths[axis] = (0, pad)
    return jnp.pad(x, widths)


def _in_proj_even_kernel(x_ref, w_ref, p_ref, qkv_ref, *, pool_width):
    xb = x_ref[...].astype(BF16)
    p_ref[...] = jnp.dot(xb, w_ref[:, :pool_width], preferred_element_type=F32)
    n_rest = (w_ref.shape[1] - pool_width) // pool_width
    for c in range(n_rest):
        lo = pool_width * (c + 1)
        qkv_ref[:, c * pool_width:(c + 1) * pool_width] = jnp.dot(
            xb, w_ref[:, lo:lo + pool_width], preferred_element_type=F32).astype(BF16)


def _in_proj_even(x2, w_in, pool_width, tm):
    m, d = x2.shape
    n = w_in.shape[1]
    blk = 2 * tm * d * 4 + 2 * d * n * 2 + 2 * tm * pool_width * 4 + 2 * tm * (n - pool_width) * 2 + tm * n * 4
    return pl.pallas_call(
        functools.partial(_in_proj_even_kernel, pool_width=pool_width),
        grid=(m // tm,),
        in_specs=[pl.BlockSpec((tm, d), lambda i: (i, 0)), _const_spec((d, n))],
        out_specs=[pl.BlockSpec((tm, pool_width), lambda i: (i, 0)),
                   pl.BlockSpec((tm, n - pool_width), lambda i: (i, 0))],
        out_shape=[jax.ShapeDtypeStruct((m, pool_width), F32),
                   jax.ShapeDtypeStruct((m, n - pool_width), BF16)],
        compiler_params=pltpu.CompilerParams(
            dimension_semantics=("arbitrary",), vmem_limit_bytes=_vmem_limit(blk)),
        name="in_proj_even",
    )(x2, w_in)


def _pool_kernel(pc_ref, pp_ref, w_ref, sc_ref, o_ref, hi_ref, lo_ref, *, ts, group_dim):
    i = pl.program_id(1)
    prev = jnp.where(i > 0, pp_ref[0], 0.0)
    h, l = _split_bf16(prev)
    hi_ref[0:BLOCK, :] = h
    lo_ref[0:BLOCK, :] = l
    h, l = _split_bf16(pc_ref[0])
    hi_ref[BLOCK:, :] = h
    lo_ref[BLOCK:, :] = l

    ii = lax.broadcasted_iota(jnp.int32, (BLOCK, 2 * BLOCK), 0) + BLOCK
    jj = lax.broadcasted_iota(jnp.int32, (BLOCK, 2 * BLOCK), 1)
    row = lax.broadcasted_iota(jnp.int32, (BLOCK, 1), 0)
    for g, w in enumerate(POOL_WINDOWS):
        band = jnp.where((jj <= ii) & (jj > ii - w), 1.0, 0.0).astype(BF16)
        cs = slice(g * group_dim, (g + 1) * group_dim)
        for r in range(ts // BLOCK):
            rs = slice(r * BLOCK, r * BLOCK + 2 * BLOCK)
            wsum = (jnp.dot(band, hi_ref[rs, cs], preferred_element_type=F32)
                    + jnp.dot(band, lo_ref[rs, cs], preferred_element_type=F32))
            t = i * ts + r * BLOCK + row
            count = jnp.minimum(t + 1, w).astype(F32)
            d = wsum / count - pc_ref[0, r * BLOCK:(r + 1) * BLOCK, cs]
            y = jnp.dot(d.astype(BF16), w_ref[g], preferred_element_type=F32) * sc_ref[:, cs]
            o_ref[0, r * BLOCK:(r + 1) * BLOCK, cs] = y.astype(o_ref.dtype)


def _pool_mixer(p, pool_w, pool_scale, ts):
    b, s, c = p.shape
    groups = len(POOL_WINDOWS)
    gd = c // groups
    per = ts // BLOCK
    blk = 2 * ts * c * 4 + 2 * BLOCK * c * 4 + 2 * groups * gd * gd * 2 + 2 * ts * c * 2 + 2 * (ts + BLOCK) * c * 2
    return pl.pallas_call(
        functools.partial(_pool_kernel, ts=ts, group_dim=gd),
        grid=(b, s // ts),
        in_specs=[pl.BlockSpec((1, ts, c), lambda bi, i: (bi, i, 0)),
                  pl.BlockSpec((1, BLOCK, c), lambda bi, i: (bi, jnp.maximum(i * per - 1, 0), 0)),
                  _const_spec((groups, gd, gd)),
                  _const_spec((1, c))],
        out_specs=pl.BlockSpec((1, ts, c), lambda bi, i: (bi, i, 0)),
        out_shape=jax.ShapeDtypeStruct((b, s, c), BF16),
        scratch_shapes=[pltpu.VMEM((ts + BLOCK, c), BF16), pltpu.VMEM((ts + BLOCK, c), BF16)],
        compiler_params=pltpu.CompilerParams(
            dimension_semantics=("arbitrary", "arbitrary"), vmem_limit_bytes=_vmem_limit(blk)),
        name="pool_mixer",
    )(p, p, pool_w, pool_scale.reshape(1, c))


def _sb_attn_kernel(q_ref, k_ref, v_ref, o_ref, *scratch, n_blocks, group, scale):
    acc_refs, carry_refs = scratch[:group], scratch[group:]
    row = lax.broadcasted_iota(jnp.int32, (BLOCK, BLOCK), 0)
    col = lax.broadcasted_iota(jnp.int32, (BLOCK, BLOCK), 1)
    causal = col < row
    jj = lax.broadcasted_iota(jnp.int32, (BLOCK, 2 * BLOCK), 0)
    ss = lax.broadcasted_iota(jnp.int32, (BLOCK, 2 * BLOCK), 1)
    tmat = jnp.where((jj > ss) | (ss >= BLOCK), 1.0, 0.0).astype(BF16)
    tmat2 = jnp.concatenate([tmat, tmat], axis=0)

    def tiles(qs, ks, vs, carries, mask):
        n = len(qs)
        log_betas, parts = [], []
        for q, kj in zip(qs, ks):
            z = lax.dot_general(q, kj, (((1,), (1,)), ((), ())), preferred_element_type=F32) * scale
            sp = jnp.maximum(z, 0.0) + jnp.log(1.0 + jnp.exp(-jnp.abs(z)))
            log_betas.append(z - sp)
            if mask is not None:
                sp = jnp.where(mask, sp, 0.0)
            parts.append(jnp.concatenate(_split_bf16(sp), axis=1))
        sums = jnp.dot(jnp.concatenate(parts, axis=0), tmat2, preferred_element_type=F32)
        pvs, new_carries = [], []
        for r in range(n):
            s = sums[r * BLOCK:(r + 1) * BLOCK]
            a = jnp.exp(log_betas[r] - s[:, :BLOCK] + carries[r])
            if mask is not None:
                a = jnp.where(mask, a, 0.0)
            pvs.append(jnp.dot(a.astype(BF16), vs[r], preferred_element_type=F32))
            new_carries.append(carries[r] - s[:, BLOCK:])
        return pvs, new_carries

    def top_of(carries):
        m = carries[0]
        for c in carries[1:]:
            m = jnp.maximum(m, c)
        return jnp.max(m)

    def q_group(gi, _):
        base = gi * group
        offs = [pl.multiple_of((base + r) * BLOCK, BLOCK) for r in range(group)]
        load = lambda ref, off: ref[0, pl.ds(off, BLOCK), :]
        pvs, carries = tiles([load(q_ref, o) for o in offs], [load(k_ref, o) for o in offs],
                             [load(v_ref, o) for o in offs], [jnp.zeros((BLOCK, BLOCK), F32)] * group, causal)
        for r in range(group):
            acc_refs[r][...] = pvs[r]
            carry_refs[r][...] = carries[r]

        def cond(c):
            k, top = c
            return jnp.logical_and(k < base + group, top > F32_EXP_UNDERFLOW)

        def body(c):
            k, _ = c
            koffs, carries = [], []
            for r in range(group):
                j = base + r - k
                carries.append(jnp.where(j >= 0, carry_refs[r][...], NO_WEIGHT))
                koffs.append(pl.multiple_of(jnp.maximum(j, 0) * BLOCK, BLOCK))
            pvs, carries = tiles([load(q_ref, o) for o in offs], [load(k_ref, o) for o in koffs],
                                 [load(v_ref, o) for o in koffs], carries, None)
            for r in range(group):
                acc_refs[r][...] += pvs[r]
                carry_refs[r][...] = carries[r]
            return k + 1, top_of(carries)

        lax.while_loop(cond, body, (jnp.int32(1), top_of(carries)))
        for r in range(group):
            o_ref[0, pl.ds(offs[r], BLOCK), :] = acc_refs[r][...].astype(o_ref.dtype)
        return 0

    lax.fori_loop(0, n_blocks // group, q_group, 0)


def _sb_attention(qkv, width):
    b, s, _ = qkv.shape
    heads = width // SB_HEAD_DIM
    n_blocks = s // BLOCK
    group = math.gcd(n_blocks, 8)
    blk = 2 * 4 * s * SB_HEAD_DIM * 2 + 2 * group * BLOCK * BLOCK * 4
    spec = lambda off: pl.BlockSpec((1, s, SB_HEAD_DIM), lambda bi, h: (bi, 0, off * heads + h))
    return pl.pallas_call(
        functools.partial(_sb_attn_kernel, n_blocks=n_blocks, group=group, scale=1.0 / math.sqrt(SB_HEAD_DIM)),
        grid=(b, heads),
        in_specs=[spec(0), spec(1), spec(2)],
        out_specs=pl.BlockSpec((1, s, SB_HEAD_DIM), lambda bi, h: (bi, 0, h)),
        out_shape=jax.ShapeDtypeStruct((b, s, width), BF16),
        scratch_shapes=[pltpu.VMEM((BLOCK, BLOCK), F32)] * (2 * group),
        compiler_params=pltpu.CompilerParams(
            dimension_semantics=("arbitrary", "arbitrary"), vmem_limit_bytes=_vmem_limit(blk)),
        name="sb_attention",
    )(qkv, qkv, qkv)


def _out_proj_ln_kernel(a1_ref, a2_ref, x_ref, w_ref, g_ref, b_ref, o_ref, *, alpha):
    k1 = a1_ref.shape[1]
    for rs in _row_parts(x_ref.shape[0]):
        y = (jnp.dot(a1_ref[rs, :], w_ref[:k1, :], preferred_element_type=F32)
             + jnp.dot(a2_ref[rs, :], w_ref[k1:, :], preferred_element_type=F32))
        o_ref[rs, :] = _layer_norm(alpha * x_ref[rs, :] + y, g_ref[...], b_ref[...])


def _out_proj_ln(a1, a2, x2, w_out, g, bvec, alpha, tm):
    m, d = x2.shape
    k1, k2 = a1.shape[1], a2.shape[1]
    blk = 2 * tm * (k1 + k2) * 2 + 2 * tm * d * 4 + 2 * (k1 + k2) * d * 2 + 2 * tm * d * 4 + tm * d * 4
    return pl.pallas_call(
        functools.partial(_out_proj_ln_kernel, alpha=alpha),
        grid=(m // tm,),
        in_specs=[pl.BlockSpec((tm, k1), lambda i: (i, 0)),
                  pl.BlockSpec((tm, k2), lambda i: (i, 0)),
                  pl.BlockSpec((tm, d), lambda i: (i, 0)),
                  _const_spec((k1 + k2, d)), _const_spec((1, d)), _const_spec((1, d))],
        out_specs=pl.BlockSpec((tm, d), lambda i: (i, 0)),
        out_shape=jax.ShapeDtypeStruct((m, d), F32),
        compiler_params=pltpu.CompilerParams(
            dimension_semantics=("arbitrary",), vmem_limit_bytes=_vmem_limit(blk)),
        name="out_proj_ln_even",
    )(a1, a2, x2, w_out, g.reshape(1, d), bvec.reshape(1, d))


def _ffn_kernel(h_ref, wg_ref, wu_ref, wd_ref, g_ref, b_ref, o_ref, hb_ref, *, alpha):
    j = pl.program_id(1)

    @pl.when(j == 0)
    def _():
        hb_ref[...] = h_ref[...].astype(BF16)
        o_ref[...] = jnp.zeros_like(o_ref)

    hb = hb_ref[...]
    gate = jnp.dot(hb, wg_ref[...], preferred_element_type=F32)
    up = jnp.dot(hb, wu_ref[...], preferred_element_type=F32)
    act = (_silu(gate) * up).astype(BF16)
    o_ref[...] += jnp.dot(act, wd_ref[...], preferred_element_type=F32)

    @pl.when(j == pl.num_programs(1) - 1)
    def _():
        o_ref[...] = _layer_norm(alpha * h_ref[...] + o_ref[...], g_ref[...], b_ref[...])


def _ffn_ln(h2, w_gu, w_down, g, bvec, alpha, tm, tf):
    m, d = h2.shape
    f = w_down.shape[0]
    nf = f // tf
    blk = 2 * tm * d * 4 + 2 * 2 * d * tf * 2 + 2 * tf * d * 2 + 2 * tm * d * 4 + tm * d * 2 + 3 * tm * tf * 4
    return pl.pallas_call(
        functools.partial(_ffn_kernel, alpha=alpha),
        grid=(m // tm, nf),
        in_specs=[pl.BlockSpec((tm, d), lambda i, j: (i, 0)),
                  pl.BlockSpec((d, tf), lambda i, j: (0, j)),
                  pl.BlockSpec((d, tf), lambda i, j: (0, j + nf)),
                  pl.BlockSpec((tf, d), lambda i, j: (j, 0)),
                  _const_spec((1, d)), _const_spec((1, d))],
        out_specs=pl.BlockSpec((tm, d), lambda i, j: (i, 0)),
        out_shape=jax.ShapeDtypeStruct((m, d), F32),
        scratch_shapes=[pltpu.VMEM((tm, d), BF16)],
        compiler_params=pltpu.CompilerParams(
            dimension_semantics=("arbitrary", "arbitrary"), vmem_limit_bytes=_vmem_limit(blk)),
        name="ffn_ln_even",
    )(h2, w_gu, w_gu, w_down, g.reshape(1, d), bvec.reshape(1, d))


def _in_proj_odd_kernel(h_ref, w_ref, g_ref, b_ref, u_ref, v_ref):
    hb = h_ref[...].astype(BF16)
    width = u_ref.shape[1]
    v = _gelu_tanh(jnp.dot(hb, w_ref[:, width:], preferred_element_type=F32))
    v_ref[...] = _layer_norm(v, g_ref[...], b_ref[...]).astype(BF16)
    for cs in _row_parts(width):
        u_ref[:, cs] = _gelu_tanh(jnp.dot(hb, w_ref[:, cs], preferred_element_type=F32))


def _in_proj_odd(h2, w_in, g, bvec, tm):
    m, d = h2.shape
    n = w_in.shape[1]
    width = n // 2
    blk = 2 * tm * d * 4 + 2 * d * n * 2 + 2 * tm * width * 4 + 2 * tm * width * 2 + 2 * tm * width * 4
    return pl.pallas_call(
        _in_proj_odd_kernel,
        grid=(m // tm,),
        in_specs=[pl.BlockSpec((tm, d), lambda i: (i, 0)), _const_spec((d, n)),
                  _const_spec((1, width)), _const_spec((1, width))],
        out_specs=[pl.BlockSpec((tm, width), lambda i: (i, 0)),
                   pl.BlockSpec((tm, width), lambda i: (i, 0))],
        out_shape=[jax.ShapeDtypeStruct((m, width), F32), jax.ShapeDtypeStruct((m, width), BF16)],
        compiler_params=pltpu.CompilerParams(
            dimension_semantics=("arbitrary",), vmem_limit_bytes=_vmem_limit(blk)),
        name="in_proj_odd",
    )(h2, w_in, g.reshape(1, width), bvec.reshape(1, width))


def _sgu_out_ln_kernel(u_ref, v_ref, ws_ref, bs_ref, h_ref, w_ref, g_ref, b_ref, o_ref, gated_ref, *, alpha):
    tm, width = u_ref.shape
    gd = width // SGU_GROUPS
    row = lax.broadcasted_iota(jnp.int32, (SGU_CHUNK, SGU_CHUNK), 0)
    col = lax.broadcasted_iota(jnp.int32, (SGU_CHUNK, SGU_CHUNK), 1)
    lower = col <= row
    for gi in range(SGU_GROUPS):
        ws = jnp.where(lower, ws_ref[gi], 0.0).astype(BF16)
        bias = bs_ref[gi]
        cs = slice(gi * gd, (gi + 1) * gd)
        for n in range(tm // SGU_CHUNK):
            rs = slice(n * SGU_CHUNK, (n + 1) * SGU_CHUNK)
            mixed = jnp.dot(ws, v_ref[rs, cs], preferred_element_type=F32) + bias
            gated_ref[rs, cs] = (u_ref[rs, cs] * mixed).astype(BF16)
    for rs in _row_parts(tm):
        y = jnp.dot(gated_ref[rs, :], w_ref[...], preferred_element_type=F32)
        o_ref[rs, :] = _layer_norm(alpha * h_ref[rs, :] + y, g_ref[...], b_ref[...])


def _sgu_out_ln(u, v, w_s, b_s, h2, w_out, g, bvec, alpha, tm):
    m, d = h2.shape
    width = u.shape[1]
    blk = (2 * tm * width * 4 + 2 * tm * width * 2 + 2 * tm * d * 4 + 2 * width * d * 2 + 2 * tm * d * 4
           + tm * width * 2 + tm * d * 4 + 4 * SGU_GROUPS * SGU_CHUNK * V7X_LANES * 4)
    return pl.pallas_call(
        functools.partial(_sgu_out_ln_kernel, alpha=alpha),
        grid=(m // tm,),
        in_specs=[pl.BlockSpec((tm, width), lambda i: (i, 0)),
                  pl.BlockSpec((tm, width), lambda i: (i, 0)),
                  _const_spec((SGU_GROUPS, SGU_CHUNK, SGU_CHUNK)),
                  _const_spec((SGU_GROUPS, SGU_CHUNK, 1)),
                  pl.BlockSpec((tm, d), lambda i: (i, 0)),
                  _const_spec((width, d)), _const_spec((1, d)), _const_spec((1, d))],
        out_specs=pl.BlockSpec((tm, d), lambda i: (i, 0)),
        out_shape=jax.ShapeDtypeStruct((m, d), F32),
        scratch_shapes=[pltpu.VMEM((tm, width), BF16)],
        compiler_params=pltpu.CompilerParams(
            dimension_semantics=("arbitrary",), vmem_limit_bytes=_vmem_limit(blk)),
        name="sgu_out_ln_odd",
    )(u, v, w_s, b_s.reshape(SGU_GROUPS, SGU_CHUNK, 1), h2, w_out, g.reshape(1, d), bvec.reshape(1, d))


META_W1, META_W2, META_D1, META_D2 = range(4)
TAB_SIZE, TAB_OFF, TAB_RUN = 0, 8, 16
TAB_DST = TAB_RUN
GROUP_ALIGN = 16
MAX_EXPERTS = 8


def _grouped_rows(tm):
    return -(-(TOP_K * tm + MAX_EXPERTS * (GROUP_ALIGN - 1)) // V7X_LANES) * V7X_LANES


def _route_kernel(h_ref, wr_ref, meta_ref, post_ref, tab_ref, run_ref, *, n_experts):
    @pl.when(pl.program_id(0) == 0)
    def _():
        run_ref[...] = jnp.zeros_like(run_ref)

    tm = h_ref.shape[0]
    h_hi, h_lo = _split_bf16(h_ref[...])
    w_hi, w_lo = _split_bf16(wr_ref[...])
    logits = (jnp.dot(h_hi, w_hi, preferred_element_type=F32) + jnp.dot(h_lo, w_hi, preferred_element_type=F32)
              + jnp.dot(h_hi, w_lo, preferred_element_type=F32))
    lanes = logits.shape[1]
    lane = lax.broadcasted_iota(jnp.int32, logits.shape, 1)
    neg = jnp.float32(-jnp.inf)
    lg = jnp.where(lane < n_experts, logits, neg)
    m1 = jnp.max(lg, axis=1, keepdims=True)
    i1 = jnp.min(jnp.where(lg == m1, lane, lanes), axis=1, keepdims=True)
    lg2 = jnp.where(lane == i1, neg, lg)
    m2 = jnp.max(lg2, axis=1, keepdims=True)
    i2 = jnp.min(jnp.where(lg2 == m2, lane, lanes), axis=1, keepdims=True)
    e2 = jnp.exp(m2 - m1)
    denom = 1.0 + e2
    w1 = 1.0 / denom
    w2 = e2 / denom

    onehot = jnp.where((lane == i1) | (lane == i2), 1.0, 0.0)
    row = lax.broadcasted_iota(jnp.int32, (tm, tm), 0)
    col = lax.broadcasted_iota(jnp.int32, (tm, tm), 1)
    earlier = jnp.where(col < row, 1.0, 0.0).astype(BF16)
    within = jnp.dot(earlier, onehot.astype(BF16), preferred_element_type=F32)
    size = jnp.sum(onehot, axis=0, keepdims=True)
    size = jnp.ceil(size * (1.0 / GROUP_ALIGN)) * GROUP_ALIGN
    er = lax.broadcasted_iota(jnp.int32, (lanes, lanes), 0)
    ec = lax.broadcasted_iota(jnp.int32, (lanes, lanes), 1)
    before = jnp.where(er < ec, 1.0, 0.0).astype(BF16)
    size8 = jnp.broadcast_to(size, (8, lanes))
    off8 = jnp.dot(size8.astype(BF16), before, preferred_element_type=F32)
    where_in_tile = within + off8[0:1, :]
    d1 = jnp.sum(jnp.where(lane == i1, where_in_tile, 0.0), axis=1, keepdims=True)
    d2 = jnp.sum(jnp.where(lane == i2, where_in_tile, 0.0), axis=1, keepdims=True)

    meta = jnp.zeros_like(logits)
    for k, val in {META_W1: w1, META_W2: w2, META_D1: d1, META_D2: d2}.items():
        meta = jnp.where(lane == k, val, meta)
    meta_ref[...] = meta
    post_ref[...] = meta.T[:post_ref.shape[0], :].astype(jnp.int32)

    run8 = jnp.broadcast_to(run_ref[...], (8, lanes))
    tab = jnp.where(lane[:8] < TAB_OFF, size8, 0.0)
    tab = tab + pltpu.roll(jnp.where(lane[:8] < n_experts, off8, 0.0), TAB_OFF, 1)
    tab = tab + pltpu.roll(jnp.where(lane[:8] < n_experts, run8, 0.0), TAB_RUN, 1)
    tab_ref[0] = tab
    run_ref[...] += size


def _moe_route(h2, w_router, tm):
    m, d = h2.shape
    n_experts = w_router.shape[1]
    assert n_experts <= MAX_EXPERTS
    w_r = _pad_to(w_router, 1, V7X_LANES)
    lanes = w_r.shape[1]
    blk = 2 * tm * d * 4 + 2 * d * lanes * 4 + 4 * tm * lanes * 4 + tm * tm * 6 + 8 * tm * lanes * 4 + tm * d * 4
    return pl.pallas_call(
        functools.partial(_route_kernel, n_experts=n_experts),
        grid=(m // tm,),
        in_specs=[pl.BlockSpec((tm, d), lambda i: (i, 0)), _const_spec((d, lanes))],
        out_specs=[pl.BlockSpec((tm, lanes), lambda i: (i, 0)),
                   pl.BlockSpec((8, tm), lambda i: (0, i)),
                   pl.BlockSpec((1, 8, lanes), lambda i: (i, 0, 0))],
        out_shape=[jax.ShapeDtypeStruct((m, lanes), F32),
                   jax.ShapeDtypeStruct((8, m), jnp.int32),
                   jax.ShapeDtypeStruct((m // tm, 8, lanes), F32)],
        scratch_shapes=[pltpu.VMEM((1, lanes), F32)],
        compiler_params=pltpu.CompilerParams(
            dimension_semantics=("arbitrary",), vmem_limit_bytes=_vmem_limit(blk)),
        name="moe_route",
    )(h2, w_r)


def _group_copy(src_ref, src_row, dst_ref, dst_row, sem):
    return pltpu.make_async_copy(src_ref.at[pl.ds(src_row, GROUP_ALIGN), :],
                                 dst_ref.at[pl.ds(dst_row, GROUP_ALIGN), :], sem)


def _for_each_group_slab(tab_ref, i, n_experts, fn):
    for e in range(n_experts):
        size, off, dst = tab_ref[i, TAB_SIZE + e], tab_ref[i, TAB_OFF + e], tab_ref[i, TAB_DST + e]

        def slab(k, _):
            step = pl.multiple_of(k * GROUP_ALIGN, GROUP_ALIGN)
            fn(pl.multiple_of(off + step, GROUP_ALIGN), pl.multiple_of(dst + step, GROUP_ALIGN))
            return 0

        lax.fori_loop(0, size // GROUP_ALIGN, slab, 0)


def _dispatch_kernel(tab_ref, fill_ref, post_ref, h_ref, xs_ref, grouped_ref, zero_ref, sems, *, n_experts):
    i = pl.program_id(0)
    last = pl.num_programs(0) - 1
    buf = i % 2
    rows = grouped_ref.shape[1]
    tm = h_ref.shape[0]
    slot = lax.broadcasted_iota(jnp.int32, (rows, tm), 0)
    d1 = post_ref[META_D1:META_D1 + 1, :]
    d2 = post_ref[META_D2:META_D2 + 1, :]
    select = jnp.where((slot == d1) | (slot == d2), 1.0, 0.0).astype(BF16)
    grouped_ref[buf] = jnp.dot(select, h_ref[...].astype(BF16), preferred_element_type=F32).astype(BF16)

    def copies(step, b, act):
        _for_each_group_slab(tab_ref, step, n_experts,
                             lambda src, dst: act(_group_copy(grouped_ref.at[b], src, xs_ref, dst, sems.at[b])))

    copies(i, buf, lambda c: c.start())

    @pl.when(i > 0)
    def _():
        copies(i - 1, 1 - buf, lambda c: c.wait())

    @pl.when(i == last)
    def _():
        copies(i, buf, lambda c: c.wait())
        zero_ref[...] = jnp.zeros_like(zero_ref)
        for e in range(fill_ref.shape[1]):
            begin = fill_ref[0, e]
            n = (fill_ref[1, e] - begin) // GROUP_ALIGN
            fill = lambda k: _group_copy(zero_ref, 0, xs_ref, pl.multiple_of(begin + k * GROUP_ALIGN, GROUP_ALIGN),
                                         sems.at[0])

            def fill_start(k, _):
                fill(k).start()
                return 0

            def fill_wait(k, _):
                fill(k).wait()
                return 0

            lax.fori_loop(0, n, fill_start, 0)
            lax.fori_loop(0, n, fill_wait, 0)


def _moe_dispatch(h2, post, tab, fill, rows, n_experts, tm):
    m, d = h2.shape
    grouped = _grouped_rows(tm)
    blk = 2 * tm * d * 4 + 2 * 8 * tm * 4 + 2 * grouped * d * 2 + grouped * d * 4 + grouped * tm * 4 + tm * d * 2
    return pl.pallas_call(
        functools.partial(_dispatch_kernel, n_experts=n_experts),
        grid=(m // tm,),
        in_specs=[pl.BlockSpec(memory_space=pltpu.SMEM),
                  pl.BlockSpec(memory_space=pltpu.SMEM),
                  pl.BlockSpec((8, tm), lambda i: (0, i)),
                  pl.BlockSpec((tm, d), lambda i: (i, 0))],
        out_specs=pl.BlockSpec(memory_space=pl.ANY),
        out_shape=jax.ShapeDtypeStruct((rows, d), BF16),
        scratch_shapes=[pltpu.VMEM((2, grouped, d), BF16), pltpu.VMEM((GROUP_ALIGN, d), BF16),
                        pltpu.SemaphoreType.DMA((2,))],
        compiler_params=pltpu.CompilerParams(
            dimension_semantics=("arbitrary",), vmem_limit_bytes=_vmem_limit(blk)),
        name="moe_dispatch",
    )(tab, fill, post, h2)


def _experts_kernel(te_ref, nt_ref, xs_ref, wgu_ref, wd_ref, ys_ref):
    occupied = pl.program_id(0) < nt_ref[0]

    @pl.when(occupied)
    def _():
        fe = wd_ref.shape[1]
        gu = jnp.dot(xs_ref[...], wgu_ref[0], preferred_element_type=F32)
        act = (_silu(gu[:, :fe]) * gu[:, fe:]).astype(BF16)
        ys_ref[...] = jnp.dot(act, wd_ref[0], preferred_element_type=F32).astype(ys_ref.dtype)

    @pl.when(jnp.logical_not(occupied))
    def _():
        ys_ref[...] = jnp.zeros_like(ys_ref)


def _moe_experts(xs, w_gu, w_d, tile_expert, n_tiles, tile):
    rows, d = xs.shape
    _, _, fe2 = w_gu.shape
    fe = w_d.shape[1]
    blk = 2 * tile * d * 2 * 2 + 2 * d * fe2 * 2 + 2 * fe * d * 2 + tile * fe2 * 4 + tile * fe * 6 + tile * d * 4
    grid_spec = pltpu.PrefetchScalarGridSpec(
        num_scalar_prefetch=2,
        grid=(rows // tile,),
        in_specs=[pl.BlockSpec((tile, d), lambda i, te, nt: (i, 0)),
                  pl.BlockSpec((1, d, fe2), lambda i, te, nt: (te[i], 0, 0)),
                  pl.BlockSpec((1, fe, d), lambda i, te, nt: (te[i], 0, 0))],
        out_specs=pl.BlockSpec((tile, d), lambda i, te, nt: (i, 0)),
    )
    return pl.pallas_call(
        _experts_kernel,
        grid_spec=grid_spec,
        out_shape=jax.ShapeDtypeStruct((rows, d), BF16),
        compiler_params=pltpu.CompilerParams(
            dimension_semantics=("arbitrary",), vmem_limit_bytes=_vmem_limit(blk)),
        name="moe_experts",
    )(tile_expert, n_tiles, xs, w_gu, w_d)


def _combine_ln_kernel(tab_ref, h_ref, meta_ref, ys_ref, g_ref, b_ref, o_ref, grouped_ref, sems, *, alpha, n_experts):
    i = pl.program_id(0)
    buf = i % 2

    def copies(step, b, act):
        _for_each_group_slab(tab_ref, step, n_experts,
                             lambda row, src: act(_group_copy(ys_ref, src, grouped_ref.at[b], row, sems.at[b])))

    @pl.when(i == 0)
    def _():
        grouped_ref[...] = jnp.zeros_like(grouped_ref)
        copies(0, 0, lambda c: c.start())

    @pl.when(i + 1 < pl.num_programs(0))
    def _():
        copies(i + 1, 1 - buf, lambda c: c.start())

    copies(i, buf, lambda c: c.wait())

    tm = h_ref.shape[0]
    rows = grouped_ref.shape[1]
    meta = meta_ref[...]
    slot = lax.broadcasted_iota(jnp.int32, (tm, rows), 1)
    grouped = grouped_ref[buf]
    moe = jnp.zeros((tm, grouped.shape[1]), F32)
    for w_lane, d_lane in ((META_W1, META_D1), (META_W2, META_D2)):
        pick = jnp.where(slot == meta[:, d_lane:d_lane + 1].astype(jnp.int32), 1.0, 0.0).astype(BF16)
        moe = moe + meta[:, w_lane:w_lane + 1] * jnp.dot(pick, grouped, preferred_element_type=F32)
    o_ref[...] = _layer_norm(alpha * h_ref[...] + moe, g_ref[...], b_ref[...])


def _moe_combine_ln(h2, tab, meta, ys, g, bvec, alpha, n_experts, tm):
    m, d = h2.shape
    lanes = meta.shape[1]
    grouped = _grouped_rows(tm)
    blk = 2 * tm * d * 4 * 2 + 2 * tm * lanes * 4 + 2 * grouped * d * 2 + 2 * tm * grouped * 2 + 3 * tm * d * 4
    return pl.pallas_call(
        functools.partial(_combine_ln_kernel, alpha=alpha, n_experts=n_experts),
        grid=(m // tm,),
        in_specs=[pl.BlockSpec(memory_space=pltpu.SMEM),
                  pl.BlockSpec((tm, d), lambda i: (i, 0)),
                  pl.BlockSpec((tm, lanes), lambda i: (i, 0)),
                  pl.BlockSpec(memory_space=pl.ANY),
                  _const_spec((1, d)), _const_spec((1, d))],
        out_specs=pl.BlockSpec((tm, d), lambda i: (i, 0)),
        out_shape=jax.ShapeDtypeStruct((m, d), F32),
        scratch_shapes=[pltpu.VMEM((2, grouped, d), BF16), pltpu.SemaphoreType.DMA((2,))],
        compiler_params=pltpu.CompilerParams(
            dimension_semantics=("arbitrary",), vmem_limit_bytes=_vmem_limit(blk)),
        name="moe_combine_ln",
    )(tab, h2, meta, ys, g.reshape(1, d), bvec.reshape(1, d))


def _moe_ln(h2, w_router, w_gu, w_d, g, bvec, alpha, tm, tile):
    m, d = h2.shape
    n_experts = w_router.shape[1]
    n_tok_tiles = m // tm
    meta, post, tab = _moe_route(h2, w_router, tm)

    tab = tab[:, 0, :].astype(jnp.int32)
    sizes = tab[:, TAB_SIZE:TAB_SIZE + n_experts]
    region = jnp.sum(sizes, axis=0)
    n_max = -(-(TOP_K * m + n_tok_tiles * n_experts * (GROUP_ALIGN - 1)) // tile) + n_experts
    rows = n_max * tile
    tiles = (region + (tile - 1)) // tile
    ends = jnp.cumsum(tiles)
    n_tiles = ends[-1]
    start = (ends - tiles) * tile
    tab = tab.at[:, TAB_DST:TAB_DST + n_experts].add(start[None, :])
    fill = jnp.stack([jnp.append(start + region, n_tiles * tile),
                      jnp.append(start + tiles * tile, rows)]).astype(jnp.int32)
    idx = jnp.minimum(jnp.arange(n_max, dtype=jnp.int32), n_tiles - 1)
    tile_expert = jnp.sum(idx[:, None] >= ends[None, :], axis=1).astype(jnp.int32)

    xs = _moe_dispatch(h2, post, tab, fill, rows, n_experts, tm)
    ys = _moe_experts(xs, w_gu, w_d, tile_expert, n_tiles.reshape(1).astype(jnp.int32), tile)
    return _moe_combine_ln(h2, tab, meta, ys, g, bvec, alpha, n_experts, tm)


def kernel(x, even_w_in, even_pool_w, even_pool_scale, even_w_out, even_ln1_g, even_ln1_b, even_ffn_w_gu, even_ffn_w_down, even_ln2_g, even_ln2_b, odd_w_in, odd_sgu_ln_g, odd_sgu_ln_b, odd_sgu_w, odd_sgu_b, odd_w_out, odd_ln1_g, odd_ln1_b, odd_router, odd_moe_w_gu, odd_moe_w_down, odd_ln2_g, odd_ln2_b):
    b, s, d = x.shape
    depth = even_w_in.shape[0] + odd_w_in.shape[0]
    alpha = (2.0 * depth) ** 0.25
    pool_width = even_pool_scale.shape[1]
    sb_width = (even_w_in.shape[2] - pool_width) // 3
    m = b * s
    tm = min(512, m)
    ffn_tm = 512
    moe_tile = 512

    h = x.reshape(m, d)
    for layer in range(depth):
        i = layer // 2
        if layer % 2 == 0:
            p, qkv = _in_proj_even(h, even_w_in[i].astype(BF16), pool_width, tm)
            pooled = _pool_mixer(p.reshape(b, s, pool_width), even_pool_w[i].astype(BF16),
                                 even_pool_scale[i], min(512, s))
            attn = _sb_attention(qkv.reshape(b, s, 3 * sb_width), sb_width)
            h = _out_proj_ln(pooled.reshape(m, pool_width), attn.reshape(m, sb_width), h,
                             even_w_out[i].astype(BF16), even_ln1_g[i], even_ln1_b[i], alpha, tm)
            h = _ffn_ln(h, even_ffn_w_gu[i].astype(BF16), even_ffn_w_down[i].astype(BF16),
                        even_ln2_g[i], even_ln2_b[i], alpha, min(ffn_tm, m), 512)
        else:
            u, v = _in_proj_odd(h, odd_w_in[i].astype(BF16), odd_sgu_ln_g[i], odd_sgu_ln_b[i], tm)
            h = _sgu_out_ln(u, v, odd_sgu_w[i], odd_sgu_b[i], h, odd_w_out[i].astype(BF16),
                            odd_ln1_g[i], odd_ln1_b[i], alpha, tm)
            h = _moe_ln(h, odd_router[i], odd_moe_w_gu[i].astype(BF16), odd_moe_w_down[i].astype(BF16),
                        odd_ln2_g[i], odd_ln2_b[i], alpha, tm, min(moe_tile, m))
    return h.reshape(b, s, d)
```

```python
import functools
import math

import jax
import jax.numpy as jnp
from jax import lax
from jax.experimental import pallas as pl
from jax.experimental.pallas import tpu as pltpu

F32 = jnp.float32
BF16 = jnp.bfloat16

LN_EPS = 1e-5
POOL_WINDOWS = (2, 4, 8, 16)
SB_HEAD_DIM = 128
SGU_GROUPS = 8
SGU_CHUNK = 128
TOP_K = 2

V7X_LANES = 128
V7X_VMEM_BYTES = 64 * 1024 * 1024
BLOCK = 128

F32_EXP_UNDERFLOW = -104.0
NO_WEIGHT = -1e30


def _vmem_limit(block_bytes):
    return int(min(block_bytes * 1.3 + (6 << 20), V7X_VMEM_BYTES - (3 << 20)))


def _layer_norm(r, g, b):
    mu = jnp.mean(r, axis=-1, keepdims=True)
    xc = r - mu
    var = jnp.mean(xc * xc, axis=-1, keepdims=True)
    return xc * lax.rsqrt(var + LN_EPS) * g + b


def _silu(x):
    return x * jax.nn.sigmoid(x)


def _gelu_tanh(x):
    c = math.sqrt(2.0 / math.pi)
    return 0.5 * x * (1.0 + jnp.tanh(c * (x + 0.044715 * (x * x * x))))


def _swiglu_chunks(xb, wg_ref, wu_ref, wd_ref, gate_col, up_col, width, chunk=256):
    y = None
    for lo in range(0, width, chunk):
        n = min(chunk, width - lo)
        gate = jnp.dot(xb, wg_ref[:, gate_col + lo:gate_col + lo + n], preferred_element_type=F32)
        up = jnp.dot(xb, wu_ref[:, up_col + lo:up_col + lo + n], preferred_element_type=F32)
        act = (_silu(gate) * up).astype(BF16)
        part = jnp.dot(act, wd_ref[lo:lo + n, :], preferred_element_type=F32)
        y = part if y is None else y + part
    return y


def _split_bf16(x):
    hi = x.astype(BF16)
    lo = (x - hi.astype(F32)).astype(BF16)
    return hi, lo


def _row_parts(rows, parts=2):
    step = rows // parts
    return [slice(k * step, (k + 1) * step) for k in range(parts)]


def _const_spec(shape):
    nd = len(shape)
    return pl.BlockSpec(shape, lambda *_: (0,) * nd)


def _pad_to(x, axis, mult):
    pad = (-x.shape[axis]) % mult
    if pad == 0:
        return x
    widths = [(0, 0)] * x.ndim
    widths[axis] = (0, pad)
    return jnp.pad(x, widths)


def _in_proj_even_kernel(x_ref, w_ref, p_ref, qkv_ref, *, pool_width):
    xb = x_ref[...].astype(BF16)
    p_ref[...] = jnp.dot(xb, w_ref[:, :pool_width], preferred_element_type=F32)
    n_rest = (w_ref.shape[1] - pool_width) // pool_width
    for c in range(n_rest):
        lo = pool_width * (c + 1)
        qkv_ref[:, c * pool_width:(c + 1) * pool_width] = jnp.dot(
            xb, w_ref[:, lo:lo + pool_width], preferred_element_type=F32).astype(BF16)


def _in_proj_even(x2, w_in, pool_width, tm):
    m, d = x2.shape
    n = w_in.shape[1]
    blk = 2 * tm * d * 4 + 2 * d * n * 2 + 2 * tm * pool_width * 4 + 2 * tm * (n - pool_width) * 2 + tm * n * 4
    return pl.pallas_call(
        functools.partial(_in_proj_even_kernel, pool_width=pool_width),
        grid=(m // tm,),
        in_specs=[pl.BlockSpec((tm, d), lambda i: (i, 0)), _const_spec((d, n))],
        out_specs=[pl.BlockSpec((tm, pool_width), lambda i: (i, 0)),
                   pl.BlockSpec((tm, n - pool_width), lambda i: (i, 0))],
        out_shape=[jax.ShapeDtypeStruct((m, pool_width), F32),
                   jax.ShapeDtypeStruct((m, n - pool_width), BF16)],
        compiler_params=pltpu.CompilerParams(
            dimension_semantics=("arbitrary",), vmem_limit_bytes=_vmem_limit(blk)),
        name="in_proj_even",
    )(x2, w_in)


def _pool_kernel(pc_ref, pp_ref, w_ref, sc_ref, o_ref, *, ts, group_dim):
    i = pl.program_id(1)
    row = lax.broadcasted_iota(jnp.int32, (ts, 1), 0)
    t = i * ts + row
    for g, w in enumerate(POOL_WINDOWS):
        cs = slice(g * group_dim, (g + 1) * group_dim)
        cur = pc_ref[0, :, cs]
        prev = jnp.where(i > 0, pp_ref[0, :, cs], 0.0)
        acc = jnp.concatenate([prev, cur], axis=0)
        k = 1
        while k < w:
            acc = acc + pltpu.roll(acc, k, 0)
            k *= 2
        count = jnp.minimum(t + 1, w).astype(F32)
        d = acc[BLOCK:] / count - cur
        y = jnp.dot(d.astype(BF16), w_ref[g], preferred_element_type=F32) * sc_ref[:, cs]
        o_ref[0, :, cs] = y.astype(o_ref.dtype)


def _pool_mixer(p, pool_w, pool_scale, ts):
    b, s, c = p.shape
    groups = len(POOL_WINDOWS)
    gd = c // groups
    per = ts // BLOCK
    blk = 2 * ts * c * 4 + 2 * BLOCK * c * 4 + 2 * groups * gd * gd * 2 + 2 * ts * c * 2 + 6 * (ts + BLOCK) * gd * 4
    return pl.pallas_call(
        functools.partial(_pool_kernel, ts=ts, group_dim=gd),
        grid=(b, s // ts),
        in_specs=[pl.BlockSpec((1, ts, c), lambda bi, i: (bi, i, 0)),
                  pl.BlockSpec((1, BLOCK, c), lambda bi, i: (bi, jnp.maximum(i * per - 1, 0), 0)),
                  _const_spec((groups, gd, gd)),
                  _const_spec((1, c))],
        out_specs=pl.BlockSpec((1, ts, c), lambda bi, i: (bi, i, 0)),
        out_shape=jax.ShapeDtypeStruct((b, s, c), BF16),
        compiler_params=pltpu.CompilerParams(
            dimension_semantics=("arbitrary", "arbitrary"), vmem_limit_bytes=_vmem_limit(blk)),
        name="pool_mixer",
    )(p, p, pool_w, pool_scale.reshape(1, c))


def _sb_attn_kernel(q_ref, k_ref, v_ref, o_ref, *scratch, n_blocks, group, scale):
    acc_refs, carry_refs = scratch[:group], scratch[group:]
    row = lax.broadcasted_iota(jnp.int32, (BLOCK, BLOCK), 0)
    col = lax.broadcasted_iota(jnp.int32, (BLOCK, BLOCK), 1)
    causal = col < row
    jj = lax.broadcasted_iota(jnp.int32, (BLOCK, 2 * BLOCK), 0)
    ss = lax.broadcasted_iota(jnp.int32, (BLOCK, 2 * BLOCK), 1)
    tmat = jnp.where((jj > ss) | (ss >= BLOCK), 1.0, 0.0).astype(BF16)
    tmat2 = jnp.concatenate([tmat, tmat], axis=0)

    def tiles(qs, ks, vs, carries, mask):
        n = len(qs)
        log_betas, parts = [], []
        for q, kj in zip(qs, ks):
            z = lax.dot_general(q, kj, (((1,), (1,)), ((), ())), preferred_element_type=F32) * scale
            sp = jnp.maximum(z, 0.0) + jnp.log(1.0 + jnp.exp(-jnp.abs(z)))
            log_betas.append(z - sp)
            if mask is not None:
                sp = jnp.where(mask, sp, 0.0)
            parts.append(jnp.concatenate(_split_bf16(sp), axis=1))
        sums = jnp.dot(jnp.concatenate(parts, axis=0), tmat2, preferred_element_type=F32)
        pvs, new_carries = [], []
        for r in range(n):
            s = sums[r * BLOCK:(r + 1) * BLOCK]
            a = jnp.exp(log_betas[r] - s[:, :BLOCK] + carries[r])
            if mask is not None:
                a = jnp.where(mask, a, 0.0)
            pvs.append(jnp.dot(a.astype(BF16), vs[r], preferred_element_type=F32))
            new_carries.append(carries[r] - s[:, BLOCK:])
        return pvs, new_carries

    def top_of(carries):
        m = carries[0]
        for c in carries[1:]:
            m = jnp.maximum(m, c)
        return jnp.max(m)

    def q_group(gi, _):
        base = gi * group
        offs = [pl.multiple_of((base + r) * BLOCK, BLOCK) for r in range(group)]
        load = lambda ref, off: ref[0, pl.ds(off, BLOCK), :]
        pvs, carries = tiles([load(q_ref, o) for o in offs], [load(k_ref, o) for o in offs],
                             [load(v_ref, o) for o in offs], [jnp.zeros((BLOCK, BLOCK), F32)] * group, causal)
        for r in range(group):
            acc_refs[r][...] = pvs[r]
            carry_refs[r][...] = carries[r]

        def cond(c):
            k, top = c
            return jnp.logical_and(k < base + group, top > F32_EXP_UNDERFLOW)

        def body(c):
            k, _ = c
            koffs, carries = [], []
            for r in range(group):
                j = base + r - k
                carries.append(jnp.where(j >= 0, carry_refs[r][...], NO_WEIGHT))
                koffs.append(pl.multiple_of(jnp.maximum(j, 0) * BLOCK, BLOCK))
            pvs, carries = tiles([load(q_ref, o) for o in offs], [load(k_ref, o) for o in koffs],
                                 [load(v_ref, o) for o in koffs], carries, None)
            for r in range(group):
                acc_refs[r][...] += pvs[r]
                carry_refs[r][...] = carries[r]
            return k + 1, top_of(carries)

        lax.while_loop(cond, body, (jnp.int32(1), top_of(carries)))
        for r in range(group):
            o_ref[0, pl.ds(offs[r], BLOCK), :] = acc_refs[r][...].astype(o_ref.dtype)
        return 0

    lax.fori_loop(0, n_blocks // group, q_group, 0)


def _sb_attention(qkv, width):
    b, s, _ = qkv.shape
    heads = width // SB_HEAD_DIM
    n_blocks = s // BLOCK
    group = math.gcd(n_blocks, 16)
    blk = 2 * 4 * s * SB_HEAD_DIM * 2 + 2 * group * BLOCK * BLOCK * 4
    spec = lambda off: pl.BlockSpec((1, s, SB_HEAD_DIM), lambda bi, h: (bi, 0, off * heads + h))
    return pl.pallas_call(
        functools.partial(_sb_attn_kernel, n_blocks=n_blocks, group=group, scale=1.0 / math.sqrt(SB_HEAD_DIM)),
        grid=(b, heads),
        in_specs=[spec(0), spec(1), spec(2)],
        out_specs=pl.BlockSpec((1, s, SB_HEAD_DIM), lambda bi, h: (bi, 0, h)),
        out_shape=jax.ShapeDtypeStruct((b, s, width), BF16),
        scratch_shapes=[pltpu.VMEM((BLOCK, BLOCK), F32)] * (2 * group),
        compiler_params=pltpu.CompilerParams(
            dimension_semantics=("arbitrary", "arbitrary"), vmem_limit_bytes=_vmem_limit(blk)),
        name="sb_attention",
    )(qkv, qkv, qkv)


def _out_proj_ln_kernel(a1_ref, a2_ref, x_ref, w_ref, g_ref, b_ref, o_ref, *, alpha):
    k1 = a1_ref.shape[1]
    for rs in _row_parts(x_ref.shape[0]):
        y = (jnp.dot(a1_ref[rs, :], w_ref[:k1, :], preferred_element_type=F32)
             + jnp.dot(a2_ref[rs, :], w_ref[k1:, :], preferred_element_type=F32))
        o_ref[rs, :] = _layer_norm(alpha * x_ref[rs, :] + y, g_ref[...], b_ref[...])


def _out_proj_ln(a1, a2, x2, w_out, g, bvec, alpha, tm):
    m, d = x2.shape
    k1, k2 = a1.shape[1], a2.shape[1]
    blk = 2 * tm * (k1 + k2) * 2 + 2 * tm * d * 4 + 2 * (k1 + k2) * d * 2 + 2 * tm * d * 4 + tm * d * 4
    return pl.pallas_call(
        functools.partial(_out_proj_ln_kernel, alpha=alpha),
        grid=(m // tm,),
        in_specs=[pl.BlockSpec((tm, k1), lambda i: (i, 0)),
                  pl.BlockSpec((tm, k2), lambda i: (i, 0)),
                  pl.BlockSpec((tm, d), lambda i: (i, 0)),
                  _const_spec((k1 + k2, d)), _const_spec((1, d)), _const_spec((1, d))],
        out_specs=pl.BlockSpec((tm, d), lambda i: (i, 0)),
        out_shape=jax.ShapeDtypeStruct((m, d), F32),
        compiler_params=pltpu.CompilerParams(
            dimension_semantics=("arbitrary",), vmem_limit_bytes=_vmem_limit(blk)),
        name="out_proj_ln_even",
    )(a1, a2, x2, w_out, g.reshape(1, d), bvec.reshape(1, d))


def _ffn_kernel(h_ref, wg_ref, wu_ref, wd_ref, g_ref, b_ref, o_ref, hb_ref, *, alpha):
    j = pl.program_id(1)

    @pl.when(j == 0)
    def _():
        hb_ref[...] = h_ref[...].astype(BF16)
        o_ref[...] = jnp.zeros_like(o_ref)

    o_ref[...] += _swiglu_chunks(hb_ref[...], wg_ref, wu_ref, wd_ref, 0, 0, wd_ref.shape[0])

    @pl.when(j == pl.num_programs(1) - 1)
    def _():
        o_ref[...] = _layer_norm(alpha * h_ref[...] + o_ref[...], g_ref[...], b_ref[...])


def _ffn_ln(h2, w_gu, w_down, g, bvec, alpha, tm, tf):
    m, d = h2.shape
    f = w_down.shape[0]
    nf = f // tf
    blk = 2 * tm * d * 4 + 2 * 2 * d * tf * 2 + 2 * tf * d * 2 + 2 * tm * d * 4 + tm * d * 2 + 3 * tm * tf * 4
    return pl.pallas_call(
        functools.partial(_ffn_kernel, alpha=alpha),
        grid=(m // tm, nf),
        in_specs=[pl.BlockSpec((tm, d), lambda i, j: (i, 0)),
                  pl.BlockSpec((d, tf), lambda i, j: (0, j)),
                  pl.BlockSpec((d, tf), lambda i, j: (0, j + nf)),
                  pl.BlockSpec((tf, d), lambda i, j: (j, 0)),
                  _const_spec((1, d)), _const_spec((1, d))],
        out_specs=pl.BlockSpec((tm, d), lambda i, j: (i, 0)),
        out_shape=jax.ShapeDtypeStruct((m, d), F32),
        scratch_shapes=[pltpu.VMEM((tm, d), BF16)],
        compiler_params=pltpu.CompilerParams(
            dimension_semantics=("arbitrary", "arbitrary"), vmem_limit_bytes=_vmem_limit(blk)),
        name="ffn_ln_even",
    )(h2, w_gu, w_gu, w_down, g.reshape(1, d), bvec.reshape(1, d))


def _in_proj_odd_kernel(h_ref, w_ref, g_ref, b_ref, u_ref, v_ref):
    hb = h_ref[...].astype(BF16)
    width = u_ref.shape[1]
    v = _gelu_tanh(jnp.dot(hb, w_ref[:, width:], preferred_element_type=F32))
    v_ref[...] = _layer_norm(v, g_ref[...], b_ref[...]).astype(BF16)
    for cs in _row_parts(width):
        u_ref[:, cs] = _gelu_tanh(jnp.dot(hb, w_ref[:, cs], preferred_element_type=F32))


def _in_proj_odd(h2, w_in, g, bvec, tm):
    m, d = h2.shape
    n = w_in.shape[1]
    width = n // 2
    blk = 2 * tm * d * 4 + 2 * d * n * 2 + 2 * tm * width * 4 + 2 * tm * width * 2 + 2 * tm * width * 4
    return pl.pallas_call(
        _in_proj_odd_kernel,
        grid=(m // tm,),
        in_specs=[pl.BlockSpec((tm, d), lambda i: (i, 0)), _const_spec((d, n)),
                  _const_spec((1, width)), _const_spec((1, width))],
        out_specs=[pl.BlockSpec((tm, width), lambda i: (i, 0)),
                   pl.BlockSpec((tm, width), lambda i: (i, 0))],
        out_shape=[jax.ShapeDtypeStruct((m, width), F32), jax.ShapeDtypeStruct((m, width), BF16)],
        compiler_params=pltpu.CompilerParams(
            dimension_semantics=("arbitrary",), vmem_limit_bytes=_vmem_limit(blk)),
        name="in_proj_odd",
    )(h2, w_in, g.reshape(1, width), bvec.reshape(1, width))


def _sgu_out_ln_kernel(u_ref, v_ref, ws_ref, bs_ref, h_ref, w_ref, g_ref, b_ref, o_ref, gated_ref, *, alpha):
    tm, width = u_ref.shape
    gd = width // SGU_GROUPS
    row = lax.broadcasted_iota(jnp.int32, (SGU_CHUNK, SGU_CHUNK), 0)
    col = lax.broadcasted_iota(jnp.int32, (SGU_CHUNK, SGU_CHUNK), 1)
    lower = col <= row
    for gi in range(SGU_GROUPS):
        ws = jnp.where(lower, ws_ref[gi], 0.0).astype(BF16)
        bias = bs_ref[gi]
        cs = slice(gi * gd, (gi + 1) * gd)
        for n in range(tm // SGU_CHUNK):
            rs = slice(n * SGU_CHUNK, (n + 1) * SGU_CHUNK)
            mixed = jnp.dot(ws, v_ref[rs, cs], preferred_element_type=F32) + bias
            gated_ref[rs, cs] = (u_ref[rs, cs] * mixed).astype(BF16)
    for rs in _row_parts(tm):
        y = jnp.dot(gated_ref[rs, :], w_ref[...], preferred_element_type=F32)
        o_ref[rs, :] = _layer_norm(alpha * h_ref[rs, :] + y, g_ref[...], b_ref[...])


def _sgu_out_ln(u, v, w_s, b_s, h2, w_out, g, bvec, alpha, tm):
    m, d = h2.shape
    width = u.shape[1]
    blk = (2 * tm * width * 4 + 2 * tm * width * 2 + 2 * tm * d * 4 + 2 * width * d * 2 + 2 * tm * d * 4
           + tm * width * 2 + tm * d * 4 + 4 * SGU_GROUPS * SGU_CHUNK * V7X_LANES * 4)
    return pl.pallas_call(
        functools.partial(_sgu_out_ln_kernel, alpha=alpha),
        grid=(m // tm,),
        in_specs=[pl.BlockSpec((tm, width), lambda i: (i, 0)),
                  pl.BlockSpec((tm, width), lambda i: (i, 0)),
                  _const_spec((SGU_GROUPS, SGU_CHUNK, SGU_CHUNK)),
                  _const_spec((SGU_GROUPS, SGU_CHUNK, 1)),
                  pl.BlockSpec((tm, d), lambda i: (i, 0)),
                  _const_spec((width, d)), _const_spec((1, d)), _const_spec((1, d))],
        out_specs=pl.BlockSpec((tm, d), lambda i: (i, 0)),
        out_shape=jax.ShapeDtypeStruct((m, d), F32),
        scratch_shapes=[pltpu.VMEM((tm, width), BF16)],
        compiler_params=pltpu.CompilerParams(
            dimension_semantics=("arbitrary",), vmem_limit_bytes=_vmem_limit(blk)),
        name="sgu_out_ln_odd",
    )(u, v, w_s, b_s.reshape(SGU_GROUPS, SGU_CHUNK, 1), h2, w_out, g.reshape(1, d), bvec.reshape(1, d))


META_W1, META_W2, META_D1, META_D2 = range(4)
TAB_SIZE, TAB_OFF, TAB_RUN = 0, 8, 16
TAB_DST = TAB_RUN
GROUP_ALIGN = 16
MAX_EXPERTS = 8


def _grouped_rows(tm):
    return -(-(TOP_K * tm + MAX_EXPERTS * (GROUP_ALIGN - 1)) // V7X_LANES) * V7X_LANES


def _route_kernel(h_ref, wr_ref, meta_ref, post_ref, tab_ref, run_ref, *, n_experts):
    @pl.when(pl.program_id(0) == 0)
    def _():
        run_ref[...] = jnp.zeros_like(run_ref)

    tm = h_ref.shape[0]
    h_hi, h_lo = _split_bf16(h_ref[...])
    w_hi, w_lo = _split_bf16(wr_ref[...])
    logits = (jnp.dot(h_hi, w_hi, preferred_element_type=F32) + jnp.dot(h_lo, w_hi, preferred_element_type=F32)
              + jnp.dot(h_hi, w_lo, preferred_element_type=F32))
    lanes = logits.shape[1]
    lane = lax.broadcasted_iota(jnp.int32, logits.shape, 1)
    neg = jnp.float32(-jnp.inf)
    lg = jnp.where(lane < n_experts, logits, neg)
    m1 = jnp.max(lg, axis=1, keepdims=True)
    i1 = jnp.min(jnp.where(lg == m1, lane, lanes), axis=1, keepdims=True)
    lg2 = jnp.where(lane == i1, neg, lg)
    m2 = jnp.max(lg2, axis=1, keepdims=True)
    i2 = jnp.min(jnp.where(lg2 == m2, lane, lanes), axis=1, keepdims=True)
    e2 = jnp.exp(m2 - m1)
    denom = 1.0 + e2
    w1 = 1.0 / denom
    w2 = e2 / denom

    onehot = jnp.where((lane == i1) | (lane == i2), 1.0, 0.0)
    row = lax.broadcasted_iota(jnp.int32, (tm, tm), 0)
    col = lax.broadcasted_iota(jnp.int32, (tm, tm), 1)
    earlier = jnp.where(col < row, 1.0, 0.0).astype(BF16)
    within = jnp.dot(earlier, onehot.astype(BF16), preferred_element_type=F32)
    size = jnp.sum(onehot, axis=0, keepdims=True)
    size = jnp.ceil(size * (1.0 / GROUP_ALIGN)) * GROUP_ALIGN
    er = lax.broadcasted_iota(jnp.int32, (lanes, lanes), 0)
    ec = lax.broadcasted_iota(jnp.int32, (lanes, lanes), 1)
    before = jnp.where(er < ec, 1.0, 0.0).astype(BF16)
    size8 = jnp.broadcast_to(size, (8, lanes))
    off8 = jnp.dot(size8.astype(BF16), before, preferred_element_type=F32)
    where_in_tile = within + off8[0:1, :]
    d1 = jnp.sum(jnp.where(lane == i1, where_in_tile, 0.0), axis=1, keepdims=True)
    d2 = jnp.sum(jnp.where(lane == i2, where_in_tile, 0.0), axis=1, keepdims=True)

    meta = jnp.zeros_like(logits)
    for k, val in {META_W1: w1, META_W2: w2, META_D1: d1, META_D2: d2}.items():
        meta = jnp.where(lane == k, val, meta)
    meta_ref[...] = meta
    post_ref[...] = meta.T[:post_ref.shape[0], :].astype(jnp.int32)

    run8 = jnp.broadcast_to(run_ref[...], (8, lanes))
    tab = jnp.where(lane[:8] < TAB_OFF, size8, 0.0)
    tab = tab + pltpu.roll(jnp.where(lane[:8] < n_experts, off8, 0.0), TAB_OFF, 1)
    tab = tab + pltpu.roll(jnp.where(lane[:8] < n_experts, run8, 0.0), TAB_RUN, 1)
    tab_ref[0] = tab
    run_ref[...] += size


def _moe_route(h2, w_router, tm):
    m, d = h2.shape
    n_experts = w_router.shape[1]
    assert n_experts <= MAX_EXPERTS
    w_r = _pad_to(w_router, 1, V7X_LANES)
    lanes = w_r.shape[1]
    blk = 2 * tm * d * 4 + 2 * d * lanes * 4 + 4 * tm * lanes * 4 + tm * tm * 6 + 8 * tm * lanes * 4 + tm * d * 4
    return pl.pallas_call(
        functools.partial(_route_kernel, n_experts=n_experts),
        grid=(m // tm,),
        in_specs=[pl.BlockSpec((tm, d), lambda i: (i, 0)), _const_spec((d, lanes))],
        out_specs=[pl.BlockSpec((tm, lanes), lambda i: (i, 0)),
                   pl.BlockSpec((8, tm), lambda i: (0, i)),
                   pl.BlockSpec((1, 8, lanes), lambda i: (i, 0, 0))],
        out_shape=[jax.ShapeDtypeStruct((m, lanes), F32),
                   jax.ShapeDtypeStruct((8, m), jnp.int32),
                   jax.ShapeDtypeStruct((m // tm, 8, lanes), F32)],
        scratch_shapes=[pltpu.VMEM((1, lanes), F32)],
        compiler_params=pltpu.CompilerParams(
            dimension_semantics=("arbitrary",), vmem_limit_bytes=_vmem_limit(blk)),
        name="moe_route",
    )(h2, w_r)


def _group_copy(src_ref, src_row, dst_ref, dst_row, sem):
    return pltpu.make_async_copy(src_ref.at[pl.ds(src_row, GROUP_ALIGN), :],
                                 dst_ref.at[pl.ds(dst_row, GROUP_ALIGN), :], sem)


def _for_each_group_slab(tab_ref, i, n_experts, fn):
    for e in range(n_experts):
        size, off, dst = tab_ref[i, TAB_SIZE + e], tab_ref[i, TAB_OFF + e], tab_ref[i, TAB_DST + e]

        def slab(k, _):
            step = pl.multiple_of(k * GROUP_ALIGN, GROUP_ALIGN)
            fn(pl.multiple_of(off + step, GROUP_ALIGN), pl.multiple_of(dst + step, GROUP_ALIGN))
            return 0

        lax.fori_loop(0, size // GROUP_ALIGN, slab, 0)


def _dispatch_kernel(tab_ref, fill_ref, post_ref, h_ref, xs_ref, grouped_ref, zero_ref, sems, *, n_experts):
    i = pl.program_id(0)
    last = pl.num_programs(0) - 1
    buf = i % 2
    rows = grouped_ref.shape[1]
    tm = h_ref.shape[0]
    slot = lax.broadcasted_iota(jnp.int32, (rows, tm), 0)
    d1 = post_ref[META_D1:META_D1 + 1, :]
    d2 = post_ref[META_D2:META_D2 + 1, :]
    select = jnp.where((slot == d1) | (slot == d2), 1.0, 0.0).astype(BF16)
    grouped_ref[buf] = jnp.dot(select, h_ref[...].astype(BF16), preferred_element_type=F32).astype(BF16)

    def copies(step, b, act):
        _for_each_group_slab(tab_ref, step, n_experts,
                             lambda src, dst: act(_group_copy(grouped_ref.at[b], src, xs_ref, dst, sems.at[b])))

    copies(i, buf, lambda c: c.start())

    @pl.when(i > 0)
    def _():
        copies(i - 1, 1 - buf, lambda c: c.wait())

    @pl.when(i == last)
    def _():
        copies(i, buf, lambda c: c.wait())
        zero_ref[...] = jnp.zeros_like(zero_ref)
        for e in range(fill_ref.shape[1]):
            begin = fill_ref[0, e]
            n = (fill_ref[1, e] - begin) // GROUP_ALIGN
            fill = lambda k: _group_copy(zero_ref, 0, xs_ref, pl.multiple_of(begin + k * GROUP_ALIGN, GROUP_ALIGN),
                                         sems.at[0])

            def fill_start(k, _):
                fill(k).start()
                return 0

            def fill_wait(k, _):
                fill(k).wait()
                return 0

            lax.fori_loop(0, n, fill_start, 0)
            lax.fori_loop(0, n, fill_wait, 0)


def _moe_dispatch(h2, post, tab, fill, rows, n_experts, tm):
    m, d = h2.shape
    grouped = _grouped_rows(tm)
    blk = 2 * tm * d * 4 + 2 * 8 * tm * 4 + 2 * grouped * d * 2 + grouped * d * 4 + grouped * tm * 4 + tm * d * 2
    return pl.pallas_call(
        functools.partial(_dispatch_kernel, n_experts=n_experts),
        grid=(m // tm,),
        in_specs=[pl.BlockSpec(memory_space=pltpu.SMEM),
                  pl.BlockSpec(memory_space=pltpu.SMEM),
                  pl.BlockSpec((8, tm), lambda i: (0, i)),
                  pl.BlockSpec((tm, d), lambda i: (i, 0))],
        out_specs=pl.BlockSpec(memory_space=pl.ANY),
        out_shape=jax.ShapeDtypeStruct((rows, d), BF16),
        scratch_shapes=[pltpu.VMEM((2, grouped, d), BF16), pltpu.VMEM((GROUP_ALIGN, d), BF16),
                        pltpu.SemaphoreType.DMA((2,))],
        compiler_params=pltpu.CompilerParams(
            dimension_semantics=("arbitrary",), vmem_limit_bytes=_vmem_limit(blk)),
        name="moe_dispatch",
    )(tab, fill, post, h2)


def _experts_kernel(te_ref, nt_ref, xs_ref, wgu_ref, wd_ref, ys_ref):
    occupied = pl.program_id(0) < nt_ref[0]

    @pl.when(occupied)
    def _():
        fe = wd_ref.shape[1]
        ys_ref[...] = _swiglu_chunks(xs_ref[...], wgu_ref.at[0], wgu_ref.at[0], wd_ref.at[0], 0, fe, fe
                                     ).astype(ys_ref.dtype)

    @pl.when(jnp.logical_not(occupied))
    def _():
        ys_ref[...] = jnp.zeros_like(ys_ref)


def _moe_experts(xs, w_gu, w_d, tile_expert, n_tiles, tile):
    rows, d = xs.shape
    _, _, fe2 = w_gu.shape
    fe = w_d.shape[1]
    blk = 2 * tile * d * 2 * 2 + 2 * d * fe2 * 2 + 2 * fe * d * 2 + tile * fe2 * 4 + tile * fe * 6 + tile * d * 4
    grid_spec = pltpu.PrefetchScalarGridSpec(
        num_scalar_prefetch=2,
        grid=(rows // tile,),
        in_specs=[pl.BlockSpec((tile, d), lambda i, te, nt: (i, 0)),
                  pl.BlockSpec((1, d, fe2), lambda i, te, nt: (te[i], 0, 0)),
                  pl.BlockSpec((1, fe, d), lambda i, te, nt: (te[i], 0, 0))],
        out_specs=pl.BlockSpec((tile, d), lambda i, te, nt: (i, 0)),
    )
    return pl.pallas_call(
        _experts_kernel,
        grid_spec=grid_spec,
        out_shape=jax.ShapeDtypeStruct((rows, d), BF16),
        compiler_params=pltpu.CompilerParams(
            dimension_semantics=("arbitrary",), vmem_limit_bytes=_vmem_limit(blk)),
        name="moe_experts",
    )(tile_expert, n_tiles, xs, w_gu, w_d)


def _combine_ln_kernel(tab_ref, h_ref, meta_ref, ys_ref, g_ref, b_ref, o_ref, grouped_ref, sems, *, alpha, n_experts):
    i = pl.program_id(0)
    buf = i % 2

    def copies(step, b, act):
        _for_each_group_slab(tab_ref, step, n_experts,
                             lambda row, src: act(_group_copy(ys_ref, src, grouped_ref.at[b], row, sems.at[b])))

    @pl.when(i == 0)
    def _():
        grouped_ref[...] = jnp.zeros_like(grouped_ref)
        copies(0, 0, lambda c: c.start())

    @pl.when(i + 1 < pl.num_programs(0))
    def _():
        copies(i + 1, 1 - buf, lambda c: c.start())

    copies(i, buf, lambda c: c.wait())

    tm = h_ref.shape[0]
    rows = grouped_ref.shape[1]
    meta = meta_ref[...]
    slot = lax.broadcasted_iota(jnp.int32, (tm, rows), 1)
    grouped = grouped_ref[buf]
    moe = jnp.zeros((tm, grouped.shape[1]), F32)
    for w_lane, d_lane in ((META_W1, META_D1), (META_W2, META_D2)):
        pick = jnp.where(slot == meta[:, d_lane:d_lane + 1].astype(jnp.int32), 1.0, 0.0).astype(BF16)
        moe = moe + meta[:, w_lane:w_lane + 1] * jnp.dot(pick, grouped, preferred_element_type=F32)
    o_ref[...] = _layer_norm(alpha * h_ref[...] + moe, g_ref[...], b_ref[...])


def _moe_combine_ln(h2, tab, meta, ys, g, bvec, alpha, n_experts, tm):
    m, d = h2.shape
    lanes = meta.shape[1]
    grouped = _grouped_rows(tm)
    blk = 2 * tm * d * 4 * 2 + 2 * tm * lanes * 4 + 2 * grouped * d * 2 + 2 * tm * grouped * 2 + 3 * tm * d * 4
    return pl.pallas_call(
        functools.partial(_combine_ln_kernel, alpha=alpha, n_experts=n_experts),
        grid=(m // tm,),
        in_specs=[pl.BlockSpec(memory_space=pltpu.SMEM),
                  pl.BlockSpec((tm, d), lambda i: (i, 0)),
                  pl.BlockSpec((tm, lanes), lambda i: (i, 0)),
                  pl.BlockSpec(memory_space=pl.ANY),
                  _const_spec((1, d)), _const_spec((1, d))],
        out_specs=pl.BlockSpec((tm, d), lambda i: (i, 0)),
        out_shape=jax.ShapeDtypeStruct((m, d), F32),
        scratch_shapes=[pltpu.VMEM((2, grouped, d), BF16), pltpu.SemaphoreType.DMA((2,))],
        compiler_params=pltpu.CompilerParams(
            dimension_semantics=("arbitrary",), vmem_limit_bytes=_vmem_limit(blk)),
        name="moe_combine_ln",
    )(tab, h2, meta, ys, g.reshape(1, d), bvec.reshape(1, d))


def _moe_ln(h2, w_router, w_gu, w_d, g, bvec, alpha, tm, tile):
    m, d = h2.shape
    n_experts = w_router.shape[1]
    n_tok_tiles = m // tm
    meta, post, tab = _moe_route(h2, w_router, tm)

    tab = tab[:, 0, :].astype(jnp.int32)
    sizes = tab[:, TAB_SIZE:TAB_SIZE + n_experts]
    region = jnp.sum(sizes, axis=0)
    n_max = -(-(TOP_K * m + n_tok_tiles * n_experts * (GROUP_ALIGN - 1)) // tile) + n_experts
    rows = n_max * tile
    tiles = (region + (tile - 1)) // tile
    ends = jnp.cumsum(tiles)
    n_tiles = ends[-1]
    start = (ends - tiles) * tile
    tab = tab.at[:, TAB_DST:TAB_DST + n_experts].add(start[None, :])
    fill = jnp.stack([jnp.append(start + region, n_tiles * tile),
                      jnp.append(start + tiles * tile, rows)]).astype(jnp.int32)
    idx = jnp.minimum(jnp.arange(n_max, dtype=jnp.int32), n_tiles - 1)
    tile_expert = jnp.sum(idx[:, None] >= ends[None, :], axis=1).astype(jnp.int32)

    xs = _moe_dispatch(h2, post, tab, fill, rows, n_experts, tm)
    ys = _moe_experts(xs, w_gu, w_d, tile_expert, n_tiles.reshape(1).astype(jnp.int32), tile)
    return _moe_combine_ln(h2, tab, meta, ys, g, bvec, alpha, n_experts, tm)


def kernel(x, even_w_in, even_pool_w, even_pool_scale, even_w_out, even_ln1_g, even_ln1_b, even_ffn_w_gu, even_ffn_w_down, even_ln2_g, even_ln2_b, odd_w_in, odd_sgu_ln_g, odd_sgu_ln_b, odd_sgu_w, odd_sgu_b, odd_w_out, odd_ln1_g, odd_ln1_b, odd_router, odd_moe_w_gu, odd_moe_w_down, odd_ln2_g, odd_ln2_b):
    b, s, d = x.shape
    depth = even_w_in.shape[0] + odd_w_in.shape[0]
    alpha = (2.0 * depth) ** 0.25
    pool_width = even_pool_scale.shape[1]
    sb_width = (even_w_in.shape[2] - pool_width) // 3
    m = b * s
    tm = min(512, m)
    ffn_tm = 512
    moe_tile = 512

    h = x.reshape(m, d)
    for layer in range(depth):
        i = layer // 2
        if layer % 2 == 0:
            p, qkv = _in_proj_even(h, even_w_in[i].astype(BF16), pool_width, tm)
            pooled = _pool_mixer(p.reshape(b, s, pool_width), even_pool_w[i].astype(BF16),
                                 even_pool_scale[i], min(512, s))
            attn = _sb_attention(qkv.reshape(b, s, 3 * sb_width), sb_width)
            h = _out_proj_ln(pooled.reshape(m, pool_width), attn.reshape(m, sb_width), h,
                             even_w_out[i].astype(BF16), even_ln1_g[i], even_ln1_b[i], alpha, tm)
            h = _ffn_ln(h, even_ffn_w_gu[i].astype(BF16), even_ffn_w_down[i].astype(BF16),
                        even_ln2_g[i], even_ln2_b[i], alpha, min(ffn_tm, m), 512)
        else:
            u, v = _in_proj_odd(h, odd_w_in[i].astype(BF16), odd_sgu_ln_g[i], odd_sgu_ln_b[i], tm)
            h = _sgu_out_ln(u, v, odd_sgu_w[i], odd_sgu_b[i], h, odd_w_out[i].astype(BF16),
                            odd_ln1_g[i], odd_ln1_b[i], alpha, tm)
            h = _moe_ln(h, odd_router[i], odd_moe_w_gu[i].astype(BF16), odd_moe_w_down[i].astype(BF16),
                        odd_ln2_g[i], odd_ln2_b[i], alpha, tm, min(moe_tile, m))
    return h.reshape(b, s, d)
```

```python
import functools
import math

import jax
import jax.numpy as jnp
from jax import lax
from jax.experimental import pallas as pl
from jax.experimental.pallas import tpu as pltpu

F32 = jnp.float32
BF16 = jnp.bfloat16

LN_EPS = 1e-5
POOL_WINDOWS = (2, 4, 8, 16)
SB_HEAD_DIM = 128
SGU_GROUPS = 8
SGU_CHUNK = 128
TOP_K = 2

V7X_LANES = 128
V7X_VMEM_BYTES = 64 * 1024 * 1024
BLOCK = 128

F32_EXP_UNDERFLOW = -104.0
NO_WEIGHT = -1e30


def _vmem_limit(block_bytes):
    return int(min(block_bytes * 1.3 + (6 << 20), V7X_VMEM_BYTES - (3 << 20)))


def _layer_norm(r, g, b):
    mu = jnp.mean(r, axis=-1, keepdims=True)
    xc = r - mu
    var = jnp.mean(xc * xc, axis=-1, keepdims=True)
    return xc * lax.rsqrt(var + LN_EPS) * g + b


def _silu(x):
    return x * jax.nn.sigmoid(x)


def _gelu_tanh(x):
    c = math.sqrt(2.0 / math.pi)
    return 0.5 * x * (1.0 + jnp.tanh(c * (x + 0.044715 * (x * x * x))))


def _swiglu_chunks(xb, wg_ref, wu_ref, wd_ref, gate_col, up_col, width, chunk=256):
    y = None
    for lo in range(0, width, chunk):
        n = min(chunk, width - lo)
        gate = jnp.dot(xb, wg_ref[:, gate_col + lo:gate_col + lo + n], preferred_element_type=F32)
        up = jnp.dot(xb, wu_ref[:, up_col + lo:up_col + lo + n], preferred_element_type=F32)
        act = (_silu(gate) * up).astype(BF16)
        part = jnp.dot(act, wd_ref[lo:lo + n, :], preferred_element_type=F32)
        y = part if y is None else y + part
    return y


def _split_bf16(x):
    hi = x.astype(BF16)
    lo = (x - hi.astype(F32)).astype(BF16)
    return hi, lo


def _row_parts(rows, parts=2):
    step = rows // parts
    return [slice(k * step, (k + 1) * step) for k in range(parts)]


def _const_spec(shape):
    nd = len(shape)
    return pl.BlockSpec(shape, lambda *_: (0,) * nd)


def _pad_to(x, axis, mult):
    pad = (-x.shape[axis]) % mult
    if pad == 0:
        return x
    widths = [(0, 0)] * x.ndim
    widths[axis] = (0, pad)
    return jnp.pad(x, widths)


def _in_proj_even_kernel(x_ref, w_ref, p_ref, qkv_ref, *, pool_width):
    xb = x_ref[...].astype(BF16)
    p_ref[...] = jnp.dot(xb, w_ref[:, :pool_width], preferred_element_type=F32)
    n_rest = (w_ref.shape[1] - pool_width) // pool_width
    for c in range(n_rest):
        lo = pool_width * (c + 1)
        qkv_ref[:, c * pool_width:(c + 1) * pool_width] = jnp.dot(
            xb, w_ref[:, lo:lo + pool_width], preferred_element_type=F32).astype(BF16)


def _in_proj_even(x2, w_in, pool_width, tm):
    m, d = x2.shape
    n = w_in.shape[1]
    blk = 2 * tm * d * 4 + 2 * d * n * 2 + 2 * tm * pool_width * 4 + 2 * tm * (n - pool_width) * 2 + tm * n * 4
    return pl.pallas_call(
        functools.partial(_in_proj_even_kernel, pool_width=pool_width),
        grid=(m // tm,),
        in_specs=[pl.BlockSpec((tm, d), lambda i: (i, 0)), _const_spec((d, n))],
        out_specs=[pl.BlockSpec((tm, pool_width), lambda i: (i, 0)),
                   pl.BlockSpec((tm, n - pool_width), lambda i: (i, 0))],
        out_shape=[jax.ShapeDtypeStruct((m, pool_width), F32),
                   jax.ShapeDtypeStruct((m, n - pool_width), BF16)],
        compiler_params=pltpu.CompilerParams(
            dimension_semantics=("arbitrary",), vmem_limit_bytes=_vmem_limit(blk)),
        name="in_proj_even",
    )(x2, w_in)


def _pool_kernel(pc_ref, pp_ref, w_ref, sc_ref, o_ref, *, ts, group_dim):
    i = pl.program_id(1)
    row = lax.broadcasted_iota(jnp.int32, (ts, 1), 0)
    t = i * ts + row
    for g, w in enumerate(POOL_WINDOWS):
        cs = slice(g * group_dim, (g + 1) * group_dim)
        cur = pc_ref[0, :, cs]
        prev = jnp.where(i > 0, pp_ref[0, :, cs], 0.0)
        acc = jnp.concatenate([prev, cur], axis=0)
        k = 1
        while k < w:
            acc = acc + pltpu.roll(acc, k, 0)
            k *= 2
        count = jnp.minimum(t + 1, w).astype(F32)
        d = acc[BLOCK:] / count - cur
        y = jnp.dot(d.astype(BF16), w_ref[g], preferred_element_type=F32) * sc_ref[:, cs]
        o_ref[0, :, cs] = y.astype(o_ref.dtype)


def _pool_mixer(p, pool_w, pool_scale, ts):
    b, s, c = p.shape
    groups = len(POOL_WINDOWS)
    gd = c // groups
    per = ts // BLOCK
    blk = 2 * ts * c * 4 + 2 * BLOCK * c * 4 + 2 * groups * gd * gd * 2 + 2 * ts * c * 2 + 6 * (ts + BLOCK) * gd * 4
    return pl.pallas_call(
        functools.partial(_pool_kernel, ts=ts, group_dim=gd),
        grid=(b, s // ts),
        in_specs=[pl.BlockSpec((1, ts, c), lambda bi, i: (bi, i, 0)),
                  pl.BlockSpec((1, BLOCK, c), lambda bi, i: (bi, jnp.maximum(i * per - 1, 0), 0)),
                  _const_spec((groups, gd, gd)),
                  _const_spec((1, c))],
        out_specs=pl.BlockSpec((1, ts, c), lambda bi, i: (bi, i, 0)),
        out_shape=jax.ShapeDtypeStruct((b, s, c), BF16),
        compiler_params=pltpu.CompilerParams(
            dimension_semantics=("arbitrary", "arbitrary"), vmem_limit_bytes=_vmem_limit(blk)),
        name="pool_mixer",
    )(p, p, pool_w, pool_scale.reshape(1, c))


def _sb_attn_kernel(q_ref, k_ref, v_ref, o_ref, *scratch, n_blocks, group, scale):
    acc_refs, carry_refs = scratch[:group], scratch[group:]
    row = lax.broadcasted_iota(jnp.int32, (BLOCK, BLOCK), 0)
    col = lax.broadcasted_iota(jnp.int32, (BLOCK, BLOCK), 1)
    causal = col < row
    jj = lax.broadcasted_iota(jnp.int32, (BLOCK, 2 * BLOCK), 0)
    ss = lax.broadcasted_iota(jnp.int32, (BLOCK, 2 * BLOCK), 1)
    tmat = jnp.where((jj > ss) | (ss >= BLOCK), 1.0, 0.0).astype(BF16)
    tmat2 = jnp.concatenate([tmat, tmat], axis=0)

    def tiles(qs, ks, vs, carries, mask):
        n = len(qs)
        log_betas, parts = [], []
        for q, kj in zip(qs, ks):
            z = lax.dot_general(q, kj, (((1,), (1,)), ((), ())), preferred_element_type=F32) * scale
            sp = jnp.maximum(z, 0.0) + jnp.log(1.0 + jnp.exp(-jnp.abs(z)))
            log_betas.append(z - sp)
            if mask is not None:
                sp = jnp.where(mask, sp, 0.0)
            parts.append(jnp.concatenate(_split_bf16(sp), axis=1))
        sums = jnp.dot(jnp.concatenate(parts, axis=0), tmat2, preferred_element_type=F32)
        pvs, new_carries = [], []
        for r in range(n):
            s = sums[r * BLOCK:(r + 1) * BLOCK]
            a = jnp.exp(log_betas[r] - s[:, :BLOCK] + carries[r])
            if mask is not None:
                a = jnp.where(mask, a, 0.0)
            pvs.append(jnp.dot(a.astype(BF16), vs[r], preferred_element_type=F32))
            new_carries.append(carries[r] - s[:, BLOCK:])
        return pvs, new_carries

    def top_of(carries):
        m = carries[0]
        for c in carries[1:]:
            m = jnp.maximum(m, c)
        return jnp.max(m)

    def q_group(gi, _):
        base = gi * group
        offs = [pl.multiple_of((base + r) * BLOCK, BLOCK) for r in range(group)]
        load = lambda ref, off: ref[0, pl.ds(off, BLOCK), :]
        pvs, carries = tiles([load(q_ref, o) for o in offs], [load(k_ref, o) for o in offs],
                             [load(v_ref, o) for o in offs], [jnp.zeros((BLOCK, BLOCK), F32)] * group, causal)
        for r in range(group):
            acc_refs[r][...] = pvs[r]
            carry_refs[r][...] = carries[r]

        def cond(c):
            k, top = c
            return jnp.logical_and(k < base + group, top > F32_EXP_UNDERFLOW)

        def body(c):
            k, _ = c
            koffs, carries = [], []
            for r in range(group):
                j = base + r - k
                carries.append(jnp.where(j >= 0, carry_refs[r][...], NO_WEIGHT))
                koffs.append(pl.multiple_of(jnp.maximum(j, 0) * BLOCK, BLOCK))
            pvs, carries = tiles([load(q_ref, o) for o in offs], [load(k_ref, o) for o in koffs],
                                 [load(v_ref, o) for o in koffs], carries, None)
            for r in range(group):
                acc_refs[r][...] += pvs[r]
                carry_refs[r][...] = carries[r]
            return k + 1, top_of(carries)

        lax.while_loop(cond, body, (jnp.int32(1), top_of(carries)))
        for r in range(group):
            o_ref[0, pl.ds(offs[r], BLOCK), :] = acc_refs[r][...].astype(o_ref.dtype)
        return 0

    lax.fori_loop(0, n_blocks // group, q_group, 0)


def _sb_attention(qkv, width):
    b, s, _ = qkv.shape
    heads = width // SB_HEAD_DIM
    n_blocks = s // BLOCK
    group = math.gcd(n_blocks, 16)
    blk = 2 * 4 * s * SB_HEAD_DIM * 2 + 2 * group * BLOCK * BLOCK * 4
    spec = lambda off: pl.BlockSpec((1, s, SB_HEAD_DIM), lambda bi, h: (bi, 0, off * heads + h))
    return pl.pallas_call(
        functools.partial(_sb_attn_kernel, n_blocks=n_blocks, group=group, scale=1.0 / math.sqrt(SB_HEAD_DIM)),
        grid=(b, heads),
        in_specs=[spec(0), spec(1), spec(2)],
        out_specs=pl.BlockSpec((1, s, SB_HEAD_DIM), lambda bi, h: (bi, 0, h)),
        out_shape=jax.ShapeDtypeStruct((b, s, width), BF16),
        scratch_shapes=[pltpu.VMEM((BLOCK, BLOCK), F32)] * (2 * group),
        compiler_params=pltpu.CompilerParams(
            dimension_semantics=("arbitrary", "arbitrary"), vmem_limit_bytes=_vmem_limit(blk)),
        name="sb_attention",
    )(qkv, qkv, qkv)


def _out_proj_ln_kernel(a1_ref, a2_ref, x_ref, w_ref, g_ref, b_ref, o_ref, *, alpha):
    k1 = a1_ref.shape[1]
    for rs in _row_parts(x_ref.shape[0]):
        y = (jnp.dot(a1_ref[rs, :], w_ref[:k1, :], preferred_element_type=F32)
             + jnp.dot(a2_ref[rs, :], w_ref[k1:, :], preferred_element_type=F32))
        o_ref[rs, :] = _layer_norm(alpha * x_ref[rs, :] + y, g_ref[...], b_ref[...])


def _out_proj_ln(a1, a2, x2, w_out, g, bvec, alpha, tm):
    m, d = x2.shape
    k1, k2 = a1.shape[1], a2.shape[1]
    blk = 2 * tm * (k1 + k2) * 2 + 2 * tm * d * 4 + 2 * (k1 + k2) * d * 2 + 2 * tm * d * 4 + tm * d * 4
    return pl.pallas_call(
        functools.partial(_out_proj_ln_kernel, alpha=alpha),
        grid=(m // tm,),
        in_specs=[pl.BlockSpec((tm, k1), lambda i: (i, 0)),
                  pl.BlockSpec((tm, k2), lambda i: (i, 0)),
                  pl.BlockSpec((tm, d), lambda i: (i, 0)),
                  _const_spec((k1 + k2, d)), _const_spec((1, d)), _const_spec((1, d))],
        out_specs=pl.BlockSpec((tm, d), lambda i: (i, 0)),
        out_shape=jax.ShapeDtypeStruct((m, d), F32),
        compiler_params=pltpu.CompilerParams(
            dimension_semantics=("arbitrary",), vmem_limit_bytes=_vmem_limit(blk)),
        name="out_proj_ln_even",
    )(a1, a2, x2, w_out, g.reshape(1, d), bvec.reshape(1, d))


def _ffn_kernel(h_ref, wg_ref, wu_ref, wd_ref, g_ref, b_ref, o_ref, hb_ref, *, alpha):
    j = pl.program_id(1)

    @pl.when(j == 0)
    def _():
        hb_ref[...] = h_ref[...].astype(BF16)
        o_ref[...] = jnp.zeros_like(o_ref)

    o_ref[...] += _swiglu_chunks(hb_ref[...], wg_ref, wu_ref, wd_ref, 0, 0, wd_ref.shape[0])

    @pl.when(j == pl.num_programs(1) - 1)
    def _():
        o_ref[...] = _layer_norm(alpha * h_ref[...] + o_ref[...], g_ref[...], b_ref[...])


def _ffn_ln(h2, w_gu, w_down, g, bvec, alpha, tm, tf):
    m, d = h2.shape
    f = w_down.shape[0]
    nf = f // tf
    blk = 2 * tm * d * 4 + 2 * 2 * d * tf * 2 + 2 * tf * d * 2 + 2 * tm * d * 4 + tm * d * 2 + 3 * tm * tf * 4
    return pl.pallas_call(
        functools.partial(_ffn_kernel, alpha=alpha),
        grid=(m // tm, nf),
        in_specs=[pl.BlockSpec((tm, d), lambda i, j: (i, 0)),
                  pl.BlockSpec((d, tf), lambda i, j: (0, j)),
                  pl.BlockSpec((d, tf), lambda i, j: (0, j + nf)),
                  pl.BlockSpec((tf, d), lambda i, j: (j, 0)),
                  _const_spec((1, d)), _const_spec((1, d))],
        out_specs=pl.BlockSpec((tm, d), lambda i, j: (i, 0)),
        out_shape=jax.ShapeDtypeStruct((m, d), F32),
        scratch_shapes=[pltpu.VMEM((tm, d), BF16)],
        compiler_params=pltpu.CompilerParams(
            dimension_semantics=("arbitrary", "arbitrary"), vmem_limit_bytes=_vmem_limit(blk)),
        name="ffn_ln_even",
    )(h2, w_gu, w_gu, w_down, g.reshape(1, d), bvec.reshape(1, d))


def _in_proj_odd_kernel(h_ref, w_ref, g_ref, b_ref, u_ref, v_ref):
    hb = h_ref[...].astype(BF16)
    width = u_ref.shape[1]
    v = _gelu_tanh(jnp.dot(hb, w_ref[:, width:], preferred_element_type=F32))
    v_ref[...] = _layer_norm(v, g_ref[...], b_ref[...]).astype(BF16)
    for cs in _row_parts(width):
        u_ref[:, cs] = _gelu_tanh(jnp.dot(hb, w_ref[:, cs], preferred_element_type=F32))


def _in_proj_odd(h2, w_in, g, bvec, tm):
    m, d = h2.shape
    n = w_in.shape[1]
    width = n // 2
    blk = 2 * tm * d * 4 + 2 * d * n * 2 + 2 * tm * width * 4 + 2 * tm * width * 2 + 2 * tm * width * 4
    return pl.pallas_call(
        _in_proj_odd_kernel,
        grid=(m // tm,),
        in_specs=[pl.BlockSpec((tm, d), lambda i: (i, 0)), _const_spec((d, n)),
                  _const_spec((1, width)), _const_spec((1, width))],
        out_specs=[pl.BlockSpec((tm, width), lambda i: (i, 0)),
                   pl.BlockSpec((tm, width), lambda i: (i, 0))],
        out_shape=[jax.ShapeDtypeStruct((m, width), F32), jax.ShapeDtypeStruct((m, width), BF16)],
        compiler_params=pltpu.CompilerParams(
            dimension_semantics=("arbitrary",), vmem_limit_bytes=_vmem_limit(blk)),
        name="in_proj_odd",
    )(h2, w_in, g.reshape(1, width), bvec.reshape(1, width))


def _sgu_out_ln_kernel(u_ref, v_ref, ws_ref, bs_ref, h_ref, w_ref, g_ref, b_ref, o_ref, gated_ref, *, alpha):
    tm, width = u_ref.shape
    gd = width // SGU_GROUPS
    row = lax.broadcasted_iota(jnp.int32, (SGU_CHUNK, SGU_CHUNK), 0)
    col = lax.broadcasted_iota(jnp.int32, (SGU_CHUNK, SGU_CHUNK), 1)
    lower = col <= row
    for gi in range(SGU_GROUPS):
        ws = jnp.where(lower, ws_ref[gi], 0.0).astype(BF16)
        bias = bs_ref[gi]
        cs = slice(gi * gd, (gi + 1) * gd)
        for n in range(tm // SGU_CHUNK):
            rs = slice(n * SGU_CHUNK, (n + 1) * SGU_CHUNK)
            mixed = jnp.dot(ws, v_ref[rs, cs], preferred_element_type=F32) + bias
            gated_ref[rs, cs] = (u_ref[rs, cs] * mixed).astype(BF16)
    for rs in _row_parts(tm):
        y = jnp.dot(gated_ref[rs, :], w_ref[...], preferred_element_type=F32)
        o_ref[rs, :] = _layer_norm(alpha * h_ref[rs, :] + y, g_ref[...], b_ref[...])


def _sgu_out_ln(u, v, w_s, b_s, h2, w_out, g, bvec, alpha, tm):
    m, d = h2.shape
    width = u.shape[1]
    blk = (2 * tm * width * 4 + 2 * tm * width * 2 + 2 * tm * d * 4 + 2 * width * d * 2 + 2 * tm * d * 4
           + tm * width * 2 + tm * d * 4 + 4 * SGU_GROUPS * SGU_CHUNK * V7X_LANES * 4)
    return pl.pallas_call(
        functools.partial(_sgu_out_ln_kernel, alpha=alpha),
        grid=(m // tm,),
        in_specs=[pl.BlockSpec((tm, width), lambda i: (i, 0)),
                  pl.BlockSpec((tm, width), lambda i: (i, 0)),
                  _const_spec((SGU_GROUPS, SGU_CHUNK, SGU_CHUNK)),
                  _const_spec((SGU_GROUPS, SGU_CHUNK, 1)),
                  pl.BlockSpec((tm, d), lambda i: (i, 0)),
                  _const_spec((width, d)), _const_spec((1, d)), _const_spec((1, d))],
        out_specs=pl.BlockSpec((tm, d), lambda i: (i, 0)),
        out_shape=jax.ShapeDtypeStruct((m, d), F32),
        scratch_shapes=[pltpu.VMEM((tm, width), BF16)],
        compiler_params=pltpu.CompilerParams(
            dimension_semantics=("arbitrary",), vmem_limit_bytes=_vmem_limit(blk)),
        name="sgu_out_ln_odd",
    )(u, v, w_s, b_s.reshape(SGU_GROUPS, SGU_CHUNK, 1), h2, w_out, g.reshape(1, d), bvec.reshape(1, d))


META_W1, META_W2, META_D1, META_D2 = range(4)
TAB_SIZE, TAB_OFF, TAB_RUN = 0, 8, 16
TAB_DST = TAB_RUN
GROUP_ALIGN = 16
MAX_EXPERTS = 8


def _grouped_rows(tm):
    return -(-(TOP_K * tm + MAX_EXPERTS * (GROUP_ALIGN - 1)) // V7X_LANES) * V7X_LANES


def _route_kernel(h_ref, wr_ref, meta_ref, post_ref, tab_ref, run_ref, *, n_experts):
    @pl.when(pl.program_id(0) == 0)
    def _():
        run_ref[...] = jnp.zeros_like(run_ref)

    tm = h_ref.shape[0]
    h_hi, h_lo = _split_bf16(h_ref[...])
    w_hi, w_lo = _split_bf16(wr_ref[...])
    logits = (jnp.dot(h_hi, w_hi, preferred_element_type=F32) + jnp.dot(h_lo, w_hi, preferred_element_type=F32)
              + jnp.dot(h_hi, w_lo, preferred_element_type=F32))
    lanes = logits.shape[1]
    lane = lax.broadcasted_iota(jnp.int32, logits.shape, 1)
    neg = jnp.float32(-jnp.inf)
    lg = jnp.where(lane < n_experts, logits, neg)
    m1 = jnp.max(lg, axis=1, keepdims=True)
    i1 = jnp.min(jnp.where(lg == m1, lane, lanes), axis=1, keepdims=True)
    lg2 = jnp.where(lane == i1, neg, lg)
    m2 = jnp.max(lg2, axis=1, keepdims=True)
    i2 = jnp.min(jnp.where(lg2 == m2, lane, lanes), axis=1, keepdims=True)
    e2 = jnp.exp(m2 - m1)
    denom = 1.0 + e2
    w1 = 1.0 / denom
    w2 = e2 / denom

    onehot = jnp.where((lane == i1) | (lane == i2), 1.0, 0.0)
    row = lax.broadcasted_iota(jnp.int32, (tm, tm), 0)
    col = lax.broadcasted_iota(jnp.int32, (tm, tm), 1)
    earlier = jnp.where(col < row, 1.0, 0.0).astype(BF16)
    within = jnp.dot(earlier, onehot.astype(BF16), preferred_element_type=F32)
    size = jnp.sum(onehot, axis=0, keepdims=True)
    size = jnp.ceil(size * (1.0 / GROUP_ALIGN)) * GROUP_ALIGN
    er = lax.broadcasted_iota(jnp.int32, (lanes, lanes), 0)
    ec = lax.broadcasted_iota(jnp.int32, (lanes, lanes), 1)
    before = jnp.where(er < ec, 1.0, 0.0).astype(BF16)
    size8 = jnp.broadcast_to(size, (8, lanes))
    off8 = jnp.dot(size8.astype(BF16), before, preferred_element_type=F32)
    where_in_tile = within + off8[0:1, :]
    d1 = jnp.sum(jnp.where(lane == i1, where_in_tile, 0.0), axis=1, keepdims=True)
    d2 = jnp.sum(jnp.where(lane == i2, where_in_tile, 0.0), axis=1, keepdims=True)

    meta = jnp.zeros_like(logits)
    for k, val in {META_W1: w1, META_W2: w2, META_D1: d1, META_D2: d2}.items():
        meta = jnp.where(lane == k, val, meta)
    meta_ref[...] = meta
    post_ref[...] = meta.T[:post_ref.shape[0], :].astype(jnp.int32)

    run8 = jnp.broadcast_to(run_ref[...], (8, lanes))
    tab = jnp.where(lane[:8] < TAB_OFF, size8, 0.0)
    tab = tab + pltpu.roll(jnp.where(lane[:8] < n_experts, off8, 0.0), TAB_OFF, 1)
    tab = tab + pltpu.roll(jnp.where(lane[:8] < n_experts, run8, 0.0), TAB_RUN, 1)
    tab_ref[0] = tab
    run_ref[...] += size


def _moe_route(h2, w_router, tm):
    m, d = h2.shape
    n_experts = w_router.shape[1]
    assert n_experts <= MAX_EXPERTS
    w_r = _pad_to(w_router, 1, V7X_LANES)
    lanes = w_r.shape[1]
    blk = 2 * tm * d * 4 + 2 * d * lanes * 4 + 4 * tm * lanes * 4 + tm * tm * 6 + 8 * tm * lanes * 4 + tm * d * 4
    return pl.pallas_call(
        functools.partial(_route_kernel, n_experts=n_experts),
        grid=(m // tm,),
        in_specs=[pl.BlockSpec((tm, d), lambda i: (i, 0)), _const_spec((d, lanes))],
        out_specs=[pl.BlockSpec((tm, lanes), lambda i: (i, 0)),
                   pl.BlockSpec((8, tm), lambda i: (0, i)),
                   pl.BlockSpec((1, 8, lanes), lambda i: (i, 0, 0))],
        out_shape=[jax.ShapeDtypeStruct((m, lanes), F32),
                   jax.ShapeDtypeStruct((8, m), jnp.int32),
                   jax.ShapeDtypeStruct((m // tm, 8, lanes), F32)],
        scratch_shapes=[pltpu.VMEM((1, lanes), F32)],
        compiler_params=pltpu.CompilerParams(
            dimension_semantics=("arbitrary",), vmem_limit_bytes=_vmem_limit(blk)),
        name="moe_route",
    )(h2, w_r)


def _group_copy(src_ref, src_row, dst_ref, dst_row, sem):
    return pltpu.make_async_copy(src_ref.at[pl.ds(src_row, GROUP_ALIGN), :],
                                 dst_ref.at[pl.ds(dst_row, GROUP_ALIGN), :], sem)


def _for_each_group_slab(tab_ref, i, n_experts, fn):
    for e in range(n_experts):
        size, off, dst = tab_ref[i, TAB_SIZE + e], tab_ref[i, TAB_OFF + e], tab_ref[i, TAB_DST + e]

        def slab(k, _):
            step = pl.multiple_of(k * GROUP_ALIGN, GROUP_ALIGN)
            fn(pl.multiple_of(off + step, GROUP_ALIGN), pl.multiple_of(dst + step, GROUP_ALIGN))
            return 0

        lax.fori_loop(0, size // GROUP_ALIGN, slab, 0)


def _split3_bf16(x):
    p0 = x.astype(BF16)
    r = x - p0.astype(F32)
    p1 = r.astype(BF16)
    p2 = (r - p1.astype(F32)).astype(BF16)
    return p0, p1, p2


def _dispatch_kernel(tab_ref, fill_ref, post_ref, h_ref, meta_ref, xs_ref, gs_ref,
                     grouped_ref, gate_ref, zero_ref, zero_gate_ref, sems, *, n_experts):
    i = pl.program_id(0)
    last = pl.num_programs(0) - 1
    buf = i % 2
    rows = grouped_ref.shape[1]
    tm = h_ref.shape[0]
    slot = lax.broadcasted_iota(jnp.int32, (rows, tm), 0)
    sel1 = jnp.where(slot == post_ref[META_D1:META_D1 + 1, :], 1.0, 0.0).astype(BF16)
    sel2 = jnp.where(slot == post_ref[META_D2:META_D2 + 1, :], 1.0, 0.0).astype(BF16)
    grouped_ref[buf] = jnp.dot(sel1 + sel2, h_ref[...].astype(BF16), preferred_element_type=F32).astype(BF16)
    meta = meta_ref[...]
    lane = lax.broadcasted_iota(jnp.int32, meta.shape, 1)
    pieces = []
    for w_lane in (META_W1, META_W2):
        wp = jnp.zeros(meta.shape, F32)
        for k, p in enumerate(_split3_bf16(meta[:, w_lane:w_lane + 1])):
            wp = jnp.where(lane == k, p.astype(F32), wp)
        pieces.append(wp.astype(BF16))
    gate_ref[buf] = (jnp.dot(sel1, pieces[0], preferred_element_type=F32)
                     + jnp.dot(sel2, pieces[1], preferred_element_type=F32))

    def copies(step, b, act):
        def both(src, dst):
            act(_group_copy(grouped_ref.at[b], src, xs_ref, dst, sems.at[b]))
            act(_group_copy(gate_ref.at[b], src, gs_ref, dst, sems.at[b]))
        _for_each_group_slab(tab_ref, step, n_experts, both)

    copies(i, buf, lambda c: c.start())

    @pl.when(i > 0)
    def _():
        copies(i - 1, 1 - buf, lambda c: c.wait())

    @pl.when(i == last)
    def _():
        copies(i, buf, lambda c: c.wait())
        zero_ref[...] = jnp.zeros_like(zero_ref)
        zero_gate_ref[...] = jnp.zeros_like(zero_gate_ref)
        for e in range(fill_ref.shape[1]):
            begin = fill_ref[0, e]
            n = (fill_ref[1, e] - begin) // GROUP_ALIGN

            def fill(k, act):
                dst = pl.multiple_of(begin + k * GROUP_ALIGN, GROUP_ALIGN)
                act(_group_copy(zero_ref, 0, xs_ref, dst, sems.at[0]))
                act(_group_copy(zero_gate_ref, 0, gs_ref, dst, sems.at[0]))
                return 0

            lax.fori_loop(0, n, lambda k, _: fill(k, lambda c: c.start()), 0)
            lax.fori_loop(0, n, lambda k, _: fill(k, lambda c: c.wait()), 0)


def _moe_dispatch(h2, post, meta, tab, fill, rows, n_experts, tm):
    m, d = h2.shape
    lanes = meta.shape[1]
    grouped = _grouped_rows(tm)
    blk = (2 * tm * d * 4 + 2 * 8 * tm * 4 + 2 * tm * lanes * 4 + 2 * grouped * d * 2 + 2 * grouped * lanes * 4
           + grouped * d * 4 + 3 * grouped * tm * 2 + tm * d * 2)
    return pl.pallas_call(
        functools.partial(_dispatch_kernel, n_experts=n_experts),
        grid=(m // tm,),
        in_specs=[pl.BlockSpec(memory_space=pltpu.SMEM),
                  pl.BlockSpec(memory_space=pltpu.SMEM),
                  pl.BlockSpec((8, tm), lambda i: (0, i)),
                  pl.BlockSpec((tm, d), lambda i: (i, 0)),
                  pl.BlockSpec((tm, lanes), lambda i: (i, 0))],
        out_specs=[pl.BlockSpec(memory_space=pl.ANY), pl.BlockSpec(memory_space=pl.ANY)],
        out_shape=[jax.ShapeDtypeStruct((rows, d), BF16), jax.ShapeDtypeStruct((rows, lanes), F32)],
        scratch_shapes=[pltpu.VMEM((2, grouped, d), BF16), pltpu.VMEM((2, grouped, lanes), F32),
                        pltpu.VMEM((GROUP_ALIGN, d), BF16), pltpu.VMEM((GROUP_ALIGN, lanes), F32),
                        pltpu.SemaphoreType.DMA((2,))],
        compiler_params=pltpu.CompilerParams(
            dimension_semantics=("arbitrary",), vmem_limit_bytes=_vmem_limit(blk)),
        name="moe_dispatch",
    )(tab, fill, post, h2, meta)


def _experts_kernel(te_ref, nt_ref, xs_ref, gs_ref, wgu_ref, wd_ref, ys_ref):
    occupied = pl.program_id(0) < nt_ref[0]

    @pl.when(occupied)
    def _():
        fe = wd_ref.shape[1]
        gate = jnp.sum(gs_ref[...], axis=1, keepdims=True)
        y = _swiglu_chunks(xs_ref[...], wgu_ref.at[0], wgu_ref.at[0], wd_ref.at[0], 0, fe, fe)
        ys_ref[...] = (gate * y).astype(ys_ref.dtype)

    @pl.when(jnp.logical_not(occupied))
    def _():
        ys_ref[...] = jnp.zeros_like(ys_ref)


def _moe_experts(xs, gs, w_gu, w_d, tile_expert, n_tiles, tile):
    rows, d = xs.shape
    lanes = gs.shape[1]
    _, _, fe2 = w_gu.shape
    fe = w_d.shape[1]
    blk = 2 * tile * d * 2 * 2 + 2 * d * fe2 * 2 + 2 * fe * d * 2 + tile * fe2 * 4 + tile * fe * 6 + tile * d * 4
    grid_spec = pltpu.PrefetchScalarGridSpec(
        num_scalar_prefetch=2,
        grid=(rows // tile,),
        in_specs=[pl.BlockSpec((tile, d), lambda i, te, nt: (i, 0)),
                  pl.BlockSpec((tile, lanes), lambda i, te, nt: (i, 0)),
                  pl.BlockSpec((1, d, fe2), lambda i, te, nt: (te[i], 0, 0)),
                  pl.BlockSpec((1, fe, d), lambda i, te, nt: (te[i], 0, 0))],
        out_specs=pl.BlockSpec((tile, d), lambda i, te, nt: (i, 0)),
    )
    return pl.pallas_call(
        _experts_kernel,
        grid_spec=grid_spec,
        out_shape=jax.ShapeDtypeStruct((rows, d), BF16),
        compiler_params=pltpu.CompilerParams(
            dimension_semantics=("arbitrary",), vmem_limit_bytes=_vmem_limit(blk)),
        name="moe_experts",
    )(tile_expert, n_tiles, xs, gs, w_gu, w_d)


def _combine_ln_kernel(tab_ref, h_ref, meta_ref, ys_ref, g_ref, b_ref, o_ref, grouped_ref, sems, *, alpha, n_experts):
    i = pl.program_id(0)
    buf = i % 2

    def copies(step, b, act):
        _for_each_group_slab(tab_ref, step, n_experts,
                             lambda row, src: act(_group_copy(ys_ref, src, grouped_ref.at[b], row, sems.at[b])))

    @pl.when(i == 0)
    def _():
        grouped_ref[...] = jnp.zeros_like(grouped_ref)
        copies(0, 0, lambda c: c.start())

    @pl.when(i + 1 < pl.num_programs(0))
    def _():
        copies(i + 1, 1 - buf, lambda c: c.start())

    copies(i, buf, lambda c: c.wait())

    tm = h_ref.shape[0]
    rows = grouped_ref.shape[1]
    meta = meta_ref[...]
    slot = lax.broadcasted_iota(jnp.int32, (tm, rows), 1)
    grouped = grouped_ref[buf]
    d1 = meta[:, META_D1:META_D1 + 1].astype(jnp.int32)
    d2 = meta[:, META_D2:META_D2 + 1].astype(jnp.int32)
    pick = jnp.where((slot == d1) | (slot == d2), 1.0, 0.0).astype(BF16)
    moe = jnp.dot(pick, grouped, preferred_element_type=F32)
    o_ref[...] = _layer_norm(alpha * h_ref[...] + moe, g_ref[...], b_ref[...])


def _moe_combine_ln(h2, tab, meta, ys, g, bvec, alpha, n_experts, tm):
    m, d = h2.shape
    lanes = meta.shape[1]
    grouped = _grouped_rows(tm)
    blk = 2 * tm * d * 4 * 2 + 2 * tm * lanes * 4 + 2 * grouped * d * 2 + 2 * tm * grouped * 2 + 3 * tm * d * 4
    return pl.pallas_call(
        functools.partial(_combine_ln_kernel, alpha=alpha, n_experts=n_experts),
        grid=(m // tm,),
        in_specs=[pl.BlockSpec(memory_space=pltpu.SMEM),
                  pl.BlockSpec((tm, d), lambda i: (i, 0)),
                  pl.BlockSpec((tm, lanes), lambda i: (i, 0)),
                  pl.BlockSpec(memory_space=pl.ANY),
                  _const_spec((1, d)), _const_spec((1, d))],
        out_specs=pl.BlockSpec((tm, d), lambda i: (i, 0)),
        out_shape=jax.ShapeDtypeStruct((m, d), F32),
        scratch_shapes=[pltpu.VMEM((2, grouped, d), BF16), pltpu.SemaphoreType.DMA((2,))],
        compiler_params=pltpu.CompilerParams(
            dimension_semantics=("arbitrary",), vmem_limit_bytes=_vmem_limit(blk)),
        name="moe_combine_ln",
    )(tab, h2, meta, ys, g.reshape(1, d), bvec.reshape(1, d))


def _moe_ln(h2, w_router, w_gu, w_d, g, bvec, alpha, tm, tile):
    m, d = h2.shape
    n_experts = w_router.shape[1]
    n_tok_tiles = m // tm
    meta, post, tab = _moe_route(h2, w_router, tm)

    tab = tab[:, 0, :].astype(jnp.int32)
    sizes = tab[:, TAB_SIZE:TAB_SIZE + n_experts]
    region = jnp.sum(sizes, axis=0)
    n_max = -(-(TOP_K * m + n_tok_tiles * n_experts * (GROUP_ALIGN - 1)) // tile) + n_experts
    rows = n_max * tile
    tiles = (region + (tile - 1)) // tile
    ends = jnp.cumsum(tiles)
    n_tiles = ends[-1]
    start = (ends - tiles) * tile
    tab = tab.at[:, TAB_DST:TAB_DST + n_experts].add(start[None, :])
    fill = jnp.stack([jnp.append(start + region, n_tiles * tile),
                      jnp.append(start + tiles * tile, rows)]).astype(jnp.int32)
    idx = jnp.minimum(jnp.arange(n_max, dtype=jnp.int32), n_tiles - 1)
    tile_expert = jnp.sum(idx[:, None] >= ends[None, :], axis=1).astype(jnp.int32)

    xs, gs = _moe_dispatch(h2, post, meta, tab, fill, rows, n_experts, tm)
    ys = _moe_experts(xs, gs, w_gu, w_d, tile_expert, n_tiles.reshape(1).astype(jnp.int32), tile)
    return _moe_combine_ln(h2, tab, meta, ys, g, bvec, alpha, n_experts, tm)


def kernel(x, even_w_in, even_pool_w, even_pool_scale, even_w_out, even_ln1_g, even_ln1_b, even_ffn_w_gu, even_ffn_w_down, even_ln2_g, even_ln2_b, odd_w_in, odd_sgu_ln_g, odd_sgu_ln_b, odd_sgu_w, odd_sgu_b, odd_w_out, odd_ln1_g, odd_ln1_b, odd_router, odd_moe_w_gu, odd_moe_w_down, odd_ln2_g, odd_ln2_b):
    b, s, d = x.shape
    depth = even_w_in.shape[0] + odd_w_in.shape[0]
    alpha = (2.0 * depth) ** 0.25
    pool_width = even_pool_scale.shape[1]
    sb_width = (even_w_in.shape[2] - pool_width) // 3
    m = b * s
    tm = min(512, m)
    ffn_tm = 512
    moe_tile = 512

    h = x.reshape(m, d)
    for layer in range(depth):
        i = layer // 2
        if layer % 2 == 0:
            p, qkv = _in_proj_even(h, even_w_in[i].astype(BF16), pool_width, tm)
            pooled = _pool_mixer(p.reshape(b, s, pool_width), even_pool_w[i].astype(BF16),
                                 even_pool_scale[i], min(512, s))
            attn = _sb_attention(qkv.reshape(b, s, 3 * sb_width), sb_width)
            h = _out_proj_ln(pooled.reshape(m, pool_width), attn.reshape(m, sb_width), h,
                             even_w_out[i].astype(BF16), even_ln1_g[i], even_ln1_b[i], alpha, tm)
            h = _ffn_ln(h, even_ffn_w_gu[i].astype(BF16), even_ffn_w_down[i].astype(BF16),
                        even_ln2_g[i], even_ln2_b[i], alpha, min(ffn_tm, m), 512)
        else:
            u, v = _in_proj_odd(h, odd_w_in[i].astype(BF16), odd_sgu_ln_g[i], odd_sgu_ln_b[i], tm)
            h = _sgu_out_ln(u, v, odd_sgu_w[i], odd_sgu_b[i], h, odd_w_out[i].astype(BF16),
                            odd_ln1_g[i], odd_ln1_b[i], alpha, tm)
            h = _moe_ln(h, odd_router[i], odd_moe_w_gu[i].astype(BF16), odd_moe_w_down[i].astype(BF16),
                        odd_ln2_g[i], odd_ln2_b[i], alpha, tm, min(moe_tile, m))
    return h.reshape(b, s, d)
```

```python
import functools
import math

import jax
import jax.numpy as jnp
from jax import lax
from jax.experimental import pallas as pl
from jax.experimental.pallas import tpu as pltpu

F32 = jnp.float32
BF16 = jnp.bfloat16

LN_EPS = 1e-5
POOL_WINDOWS = (2, 4, 8, 16)
SB_HEAD_DIM = 128
SGU_GROUPS = 8
SGU_CHUNK = 128
TOP_K = 2

V7X_LANES = 128
V7X_VMEM_BYTES = 64 * 1024 * 1024
BLOCK = 128

F32_EXP_UNDERFLOW = -104.0
NO_WEIGHT = -1e30


def _vmem_limit(block_bytes):
    return int(min(block_bytes * 1.3 + (6 << 20), V7X_VMEM_BYTES - (3 << 20)))


def _layer_norm(r, g, b):
    mu = jnp.mean(r, axis=-1, keepdims=True)
    xc = r - mu
    var = jnp.mean(xc * xc, axis=-1, keepdims=True)
    return xc * lax.rsqrt(var + LN_EPS) * g + b


def _silu(x):
    return x * jax.nn.sigmoid(x)


def _gelu_tanh(x):
    c = math.sqrt(2.0 / math.pi)
    return 0.5 * x * (1.0 + jnp.tanh(c * (x + 0.044715 * (x * x * x))))


def _swiglu_chunks(xb, wg_ref, wu_ref, wd_ref, gate_col, up_col, width, chunk=256):
    y = None
    for lo in range(0, width, chunk):
        n = min(chunk, width - lo)
        gate = jnp.dot(xb, wg_ref[:, gate_col + lo:gate_col + lo + n], preferred_element_type=F32)
        up = jnp.dot(xb, wu_ref[:, up_col + lo:up_col + lo + n], preferred_element_type=F32)
        act = (_silu(gate) * up).astype(BF16)
        part = jnp.dot(act, wd_ref[lo:lo + n, :], preferred_element_type=F32)
        y = part if y is None else y + part
    return y


def _split_bf16(x):
    hi = x.astype(BF16)
    lo = (x - hi.astype(F32)).astype(BF16)
    return hi, lo


def _even_slices(rows, parts=2):
    step = rows // parts
    return [slice(k * step, (k + 1) * step) for k in range(parts)]


def _const_spec(shape):
    nd = len(shape)
    return pl.BlockSpec(shape, lambda *_: (0,) * nd)


def _pad_to(x, axis, mult):
    pad = (-x.shape[axis]) % mult
    if pad == 0:
        return x
    widths = [(0, 0)] * x.ndim
    widths[axis] = (0, pad)
    return jnp.pad(x, widths)


def _in_proj_even_kernel(x_ref, w_ref, p_ref, qkv_ref, *, pool_width):
    xb = x_ref[...].astype(BF16)
    p_ref[...] = jnp.dot(xb, w_ref[:, :pool_width], preferred_element_type=F32)
    n_rest = (w_ref.shape[1] - pool_width) // pool_width
    for c in range(n_rest):
        lo = pool_width * (c + 1)
        qkv_ref[:, c * pool_width:(c + 1) * pool_width] = jnp.dot(
            xb, w_ref[:, lo:lo + pool_width], preferred_element_type=F32).astype(BF16)


def _in_proj_even(x2, w_in, pool_width, tm):
    m, d = x2.shape
    n = w_in.shape[1]
    blk = 2 * tm * d * 4 + 2 * d * n * 2 + 2 * tm * pool_width * 4 + 2 * tm * (n - pool_width) * 2 + tm * n * 4
    return pl.pallas_call(
        functools.partial(_in_proj_even_kernel, pool_width=pool_width),
        grid=(m // tm,),
        in_specs=[pl.BlockSpec((tm, d), lambda i: (i, 0)), _const_spec((d, n))],
        out_specs=[pl.BlockSpec((tm, pool_width), lambda i: (i, 0)),
                   pl.BlockSpec((tm, n - pool_width), lambda i: (i, 0))],
        out_shape=[jax.ShapeDtypeStruct((m, pool_width), F32),
                   jax.ShapeDtypeStruct((m, n - pool_width), BF16)],
        compiler_params=pltpu.CompilerParams(
            dimension_semantics=("arbitrary",), vmem_limit_bytes=_vmem_limit(blk)),
        name="in_proj_even",
    )(x2, w_in)


def _pool_kernel(pc_ref, pp_ref, w_ref, sc_ref, o_ref, *, ts, group_dim):
    i = pl.program_id(1)
    row = lax.broadcasted_iota(jnp.int32, (ts, 1), 0)
    t = i * ts + row
    for g, w in enumerate(POOL_WINDOWS):
        cs = slice(g * group_dim, (g + 1) * group_dim)
        cur = pc_ref[0, :, cs]
        prev = jnp.where(i > 0, pp_ref[0, :, cs], 0.0)
        acc = jnp.concatenate([prev, cur], axis=0)
        k = 1
        while k < w:
            acc = acc + pltpu.roll(acc, k, 0)
            k *= 2
        count = jnp.minimum(t + 1, w).astype(F32)
        d = acc[BLOCK:] / count - cur
        y = jnp.dot(d.astype(BF16), w_ref[g], preferred_element_type=F32) * sc_ref[:, cs]
        o_ref[0, :, cs] = y.astype(o_ref.dtype)


def _pool_mixer(p, pool_w, pool_scale, ts):
    b, s, c = p.shape
    groups = len(POOL_WINDOWS)
    gd = c // groups
    per = ts // BLOCK
    blk = 2 * ts * c * 4 + 2 * BLOCK * c * 4 + 2 * groups * gd * gd * 2 + 2 * ts * c * 2 + 6 * (ts + BLOCK) * gd * 4
    return pl.pallas_call(
        functools.partial(_pool_kernel, ts=ts, group_dim=gd),
        grid=(b, s // ts),
        in_specs=[pl.BlockSpec((1, ts, c), lambda bi, i: (bi, i, 0)),
                  pl.BlockSpec((1, BLOCK, c), lambda bi, i: (bi, jnp.maximum(i * per - 1, 0), 0)),
                  _const_spec((groups, gd, gd)),
                  _const_spec((1, c))],
        out_specs=pl.BlockSpec((1, ts, c), lambda bi, i: (bi, i, 0)),
        out_shape=jax.ShapeDtypeStruct((b, s, c), BF16),
        compiler_params=pltpu.CompilerParams(
            dimension_semantics=("arbitrary", "arbitrary"), vmem_limit_bytes=_vmem_limit(blk)),
        name="pool_mixer",
    )(p, p, pool_w, pool_scale.reshape(1, c))


def _sb_attn_kernel(q_ref, k_ref, v_ref, o_ref, *scratch, n_blocks, group, scale):
    acc_refs, carry_refs = scratch[:group], scratch[group:]
    row = lax.broadcasted_iota(jnp.int32, (BLOCK, BLOCK), 0)
    col = lax.broadcasted_iota(jnp.int32, (BLOCK, BLOCK), 1)
    causal = col < row
    jj = lax.broadcasted_iota(jnp.int32, (BLOCK, 2 * BLOCK), 0)
    ss = lax.broadcasted_iota(jnp.int32, (BLOCK, 2 * BLOCK), 1)
    tmat = jnp.where((jj > ss) | (ss >= BLOCK), 1.0, 0.0).astype(BF16)
    tmat2 = jnp.concatenate([tmat, tmat], axis=0)

    def tiles(qs, ks, vs, carries, mask):
        n = len(qs)
        log_betas, parts = [], []
        for q, kj in zip(qs, ks):
            z = lax.dot_general(q, kj, (((1,), (1,)), ((), ())), preferred_element_type=F32) * scale
            sp = jnp.maximum(z, 0.0) + jnp.log(1.0 + jnp.exp(-jnp.abs(z)))
            log_betas.append(z - sp)
            if mask is not None:
                sp = jnp.where(mask, sp, 0.0)
            parts.append(jnp.concatenate(_split_bf16(sp), axis=1))
        sums = jnp.dot(jnp.concatenate(parts, axis=0), tmat2, preferred_element_type=F32)
        pvs, new_carries = [], []
        for r in range(n):
            s = sums[r * BLOCK:(r + 1) * BLOCK]
            a = jnp.exp(log_betas[r] - s[:, :BLOCK] + carries[r])
            if mask is not None:
                a = jnp.where(mask, a, 0.0)
            pvs.append(jnp.dot(a.astype(BF16), vs[r], preferred_element_type=F32))
            new_carries.append(carries[r] - s[:, BLOCK:])
        return pvs, new_carries

    def top_of(carries):
        m = carries[0]
        for c in carries[1:]:
            m = jnp.maximum(m, c)
        return jnp.max(m)

    def q_group(gi, _):
        base = gi * group
        offs = [pl.multiple_of((base + r) * BLOCK, BLOCK) for r in range(group)]
        load = lambda ref, off: ref[0, pl.ds(off, BLOCK), :]
        pvs, carries = tiles([load(q_ref, o) for o in offs], [load(k_ref, o) for o in offs],
                             [load(v_ref, o) for o in offs], [jnp.zeros((BLOCK, BLOCK), F32)] * group, causal)
        for r in range(group):
            acc_refs[r][...] = pvs[r]
            carry_refs[r][...] = carries[r]

        def cond(c):
            k, top = c
            return jnp.logical_and(k < base + group, top > F32_EXP_UNDERFLOW)

        def body(c):
            k, _ = c
            koffs, carries = [], []
            for r in range(group):
                j = base + r - k
                carries.append(jnp.where(j >= 0, carry_refs[r][...], NO_WEIGHT))
                koffs.append(pl.multiple_of(jnp.maximum(j, 0) * BLOCK, BLOCK))
            pvs, carries = tiles([load(q_ref, o) for o in offs], [load(k_ref, o) for o in koffs],
                                 [load(v_ref, o) for o in koffs], carries, None)
            for r in range(group):
                acc_refs[r][...] += pvs[r]
                carry_refs[r][...] = carries[r]
            return k + 1, top_of(carries)

        lax.while_loop(cond, body, (jnp.int32(1), top_of(carries)))
        for r in range(group):
            o_ref[0, pl.ds(offs[r], BLOCK), :] = acc_refs[r][...].astype(o_ref.dtype)
        return 0

    lax.fori_loop(0, n_blocks // group, q_group, 0)


def _sb_attention(qkv, width):
    b, s, _ = qkv.shape
    heads = width // SB_HEAD_DIM
    n_blocks = s // BLOCK
    group = math.gcd(n_blocks, 16)
    blk = 2 * 4 * s * SB_HEAD_DIM * 2 + 2 * group * BLOCK * BLOCK * 4
    spec = lambda off: pl.BlockSpec((1, s, SB_HEAD_DIM), lambda bi, h: (bi, 0, off * heads + h))
    return pl.pallas_call(
        functools.partial(_sb_attn_kernel, n_blocks=n_blocks, group=group, scale=1.0 / math.sqrt(SB_HEAD_DIM)),
        grid=(b, heads),
        in_specs=[spec(0), spec(1), spec(2)],
        out_specs=pl.BlockSpec((1, s, SB_HEAD_DIM), lambda bi, h: (bi, 0, h)),
        out_shape=jax.ShapeDtypeStruct((b, s, width), BF16),
        scratch_shapes=[pltpu.VMEM((BLOCK, BLOCK), F32)] * (2 * group),
        compiler_params=pltpu.CompilerParams(
            dimension_semantics=("arbitrary", "arbitrary"), vmem_limit_bytes=_vmem_limit(blk)),
        name="sb_attention",
    )(qkv, qkv, qkv)


def _out_proj_ln_kernel(a1_ref, a2_ref, x_ref, w_ref, g_ref, b_ref, o_ref, *, alpha):
    k1 = a1_ref.shape[1]
    for rs in _even_slices(x_ref.shape[0]):
        y = (jnp.dot(a1_ref[rs, :], w_ref[:k1, :], preferred_element_type=F32)
             + jnp.dot(a2_ref[rs, :], w_ref[k1:, :], preferred_element_type=F32))
        o_ref[rs, :] = _layer_norm(alpha * x_ref[rs, :] + y, g_ref[...], b_ref[...])


def _out_proj_ln(a1, a2, x2, w_out, g, bvec, alpha, tm):
    m, d = x2.shape
    k1, k2 = a1.shape[1], a2.shape[1]
    blk = 2 * tm * (k1 + k2) * 2 + 2 * tm * d * 4 + 2 * (k1 + k2) * d * 2 + 2 * tm * d * 4 + tm * d * 4
    return pl.pallas_call(
        functools.partial(_out_proj_ln_kernel, alpha=alpha),
        grid=(m // tm,),
        in_specs=[pl.BlockSpec((tm, k1), lambda i: (i, 0)),
                  pl.BlockSpec((tm, k2), lambda i: (i, 0)),
                  pl.BlockSpec((tm, d), lambda i: (i, 0)),
                  _const_spec((k1 + k2, d)), _const_spec((1, d)), _const_spec((1, d))],
        out_specs=pl.BlockSpec((tm, d), lambda i: (i, 0)),
        out_shape=jax.ShapeDtypeStruct((m, d), F32),
        compiler_params=pltpu.CompilerParams(
            dimension_semantics=("arbitrary",), vmem_limit_bytes=_vmem_limit(blk)),
        name="out_proj_ln_even",
    )(a1, a2, x2, w_out, g.reshape(1, d), bvec.reshape(1, d))


def _ffn_kernel(h_ref, wg_ref, wu_ref, wd_ref, g_ref, b_ref, o_ref, hb_ref, *, alpha):
    j = pl.program_id(1)

    @pl.when(j == 0)
    def _():
        hb_ref[...] = h_ref[...].astype(BF16)
        o_ref[...] = jnp.zeros_like(o_ref)

    o_ref[...] += _swiglu_chunks(hb_ref[...], wg_ref, wu_ref, wd_ref, 0, 0, wd_ref.shape[0])

    @pl.when(j == pl.num_programs(1) - 1)
    def _():
        o_ref[...] = _layer_norm(alpha * h_ref[...] + o_ref[...], g_ref[...], b_ref[...])


def _ffn_ln(h2, w_gu, w_down, g, bvec, alpha, tm, tf):
    m, d = h2.shape
    f = w_down.shape[0]
    nf = f // tf
    blk = 2 * tm * d * 4 + 2 * 2 * d * tf * 2 + 2 * tf * d * 2 + 2 * tm * d * 4 + tm * d * 2 + 3 * tm * tf * 4
    return pl.pallas_call(
        functools.partial(_ffn_kernel, alpha=alpha),
        grid=(m // tm, nf),
        in_specs=[pl.BlockSpec((tm, d), lambda i, j: (i, 0)),
                  pl.BlockSpec((d, tf), lambda i, j: (0, j)),
                  pl.BlockSpec((d, tf), lambda i, j: (0, j + nf)),
                  pl.BlockSpec((tf, d), lambda i, j: (j, 0)),
                  _const_spec((1, d)), _const_spec((1, d))],
        out_specs=pl.BlockSpec((tm, d), lambda i, j: (i, 0)),
        out_shape=jax.ShapeDtypeStruct((m, d), F32),
        scratch_shapes=[pltpu.VMEM((tm, d), BF16)],
        compiler_params=pltpu.CompilerParams(
            dimension_semantics=("arbitrary", "arbitrary"), vmem_limit_bytes=_vmem_limit(blk)),
        name="ffn_ln_even",
    )(h2, w_gu, w_gu, w_down, g.reshape(1, d), bvec.reshape(1, d))


def _in_proj_odd_kernel(h_ref, w_ref, g_ref, b_ref, u_ref, v_ref):
    hb = h_ref[...].astype(BF16)
    width = u_ref.shape[1]
    v = _gelu_tanh(jnp.dot(hb, w_ref[:, width:], preferred_element_type=F32))
    v_ref[...] = _layer_norm(v, g_ref[...], b_ref[...]).astype(BF16)
    for cs in _even_slices(width):
        u_ref[:, cs] = _gelu_tanh(jnp.dot(hb, w_ref[:, cs], preferred_element_type=F32))


def _in_proj_odd(h2, w_in, g, bvec, tm):
    m, d = h2.shape
    n = w_in.shape[1]
    width = n // 2
    blk = 2 * tm * d * 4 + 2 * d * n * 2 + 2 * tm * width * 4 + 2 * tm * width * 2 + 2 * tm * width * 4
    return pl.pallas_call(
        _in_proj_odd_kernel,
        grid=(m // tm,),
        in_specs=[pl.BlockSpec((tm, d), lambda i: (i, 0)), _const_spec((d, n)),
                  _const_spec((1, width)), _const_spec((1, width))],
        out_specs=[pl.BlockSpec((tm, width), lambda i: (i, 0)),
                   pl.BlockSpec((tm, width), lambda i: (i, 0))],
        out_shape=[jax.ShapeDtypeStruct((m, width), F32), jax.ShapeDtypeStruct((m, width), BF16)],
        compiler_params=pltpu.CompilerParams(
            dimension_semantics=("arbitrary",), vmem_limit_bytes=_vmem_limit(blk)),
        name="in_proj_odd",
    )(h2, w_in, g.reshape(1, width), bvec.reshape(1, width))


def _sgu_out_ln_kernel(u_ref, v_ref, ws_ref, bs_ref, h_ref, w_ref, g_ref, b_ref, o_ref, gated_ref, *, alpha):
    tm, width = u_ref.shape
    gd = width // SGU_GROUPS
    row = lax.broadcasted_iota(jnp.int32, (SGU_CHUNK, SGU_CHUNK), 0)
    col = lax.broadcasted_iota(jnp.int32, (SGU_CHUNK, SGU_CHUNK), 1)
    lower = col <= row
    for gi in range(SGU_GROUPS):
        ws = jnp.where(lower, ws_ref[gi], 0.0).astype(BF16)
        bias = bs_ref[gi]
        cs = slice(gi * gd, (gi + 1) * gd)
        for n in range(tm // SGU_CHUNK):
            rs = slice(n * SGU_CHUNK, (n + 1) * SGU_CHUNK)
            mixed = jnp.dot(ws, v_ref[rs, cs], preferred_element_type=F32) + bias
            gated_ref[rs, cs] = (u_ref[rs, cs] * mixed).astype(BF16)
    for rs in _even_slices(tm):
        y = jnp.dot(gated_ref[rs, :], w_ref[...], preferred_element_type=F32)
        o_ref[rs, :] = _layer_norm(alpha * h_ref[rs, :] + y, g_ref[...], b_ref[...])


def _sgu_out_ln(u, v, w_s, b_s, h2, w_out, g, bvec, alpha, tm):
    m, d = h2.shape
    width = u.shape[1]
    blk = (2 * tm * width * 4 + 2 * tm * width * 2 + 2 * tm * d * 4 + 2 * width * d * 2 + 2 * tm * d * 4
           + tm * width * 2 + tm * d * 4 + 4 * SGU_GROUPS * SGU_CHUNK * V7X_LANES * 4)
    return pl.pallas_call(
        functools.partial(_sgu_out_ln_kernel, alpha=alpha),
        grid=(m // tm,),
        in_specs=[pl.BlockSpec((tm, width), lambda i: (i, 0)),
                  pl.BlockSpec((tm, width), lambda i: (i, 0)),
                  _const_spec((SGU_GROUPS, SGU_CHUNK, SGU_CHUNK)),
                  _const_spec((SGU_GROUPS, SGU_CHUNK, 1)),
                  pl.BlockSpec((tm, d), lambda i: (i, 0)),
                  _const_spec((width, d)), _const_spec((1, d)), _const_spec((1, d))],
        out_specs=pl.BlockSpec((tm, d), lambda i: (i, 0)),
        out_shape=jax.ShapeDtypeStruct((m, d), F32),
        scratch_shapes=[pltpu.VMEM((tm, width), BF16)],
        compiler_params=pltpu.CompilerParams(
            dimension_semantics=("arbitrary",), vmem_limit_bytes=_vmem_limit(blk)),
        name="sgu_out_ln_odd",
    )(u, v, w_s, b_s.reshape(SGU_GROUPS, SGU_CHUNK, 1), h2, w_out, g.reshape(1, d), bvec.reshape(1, d))


META_W1, META_W2, META_D1, META_D2 = range(4)
TAB_SIZE, TAB_OFF, TAB_RUN = 0, 8, 16
TAB_DST = TAB_RUN
GROUP_ALIGN = 16
MAX_EXPERTS = 8


def _grouped_rows(tm):
    return -(-(TOP_K * tm + MAX_EXPERTS * (GROUP_ALIGN - 1)) // V7X_LANES) * V7X_LANES


def _route_kernel(h_ref, wr_ref, meta_ref, post_ref, tab_ref, run_ref, *, n_experts):
    @pl.when(pl.program_id(0) == 0)
    def _():
        run_ref[...] = jnp.zeros_like(run_ref)

    tm = h_ref.shape[0]
    lanes = wr_ref.shape[1]
    parts = jnp.dot(jnp.concatenate(_split_bf16(h_ref[...]), axis=0),
                    jnp.concatenate(_split_bf16(wr_ref[...]), axis=1), preferred_element_type=F32)
    logits = (parts[:tm, :lanes] + parts[:tm, lanes:]) + (parts[tm:, :lanes] + parts[tm:, lanes:])
    lane = lax.broadcasted_iota(jnp.int32, logits.shape, 1)
    neg = jnp.float32(-jnp.inf)
    lg = jnp.where(lane < n_experts, logits, neg)
    m1 = jnp.max(lg, axis=1, keepdims=True)
    i1 = jnp.min(jnp.where(lg == m1, lane, lanes), axis=1, keepdims=True)
    lg2 = jnp.where(lane == i1, neg, lg)
    m2 = jnp.max(lg2, axis=1, keepdims=True)
    i2 = jnp.min(jnp.where(lg2 == m2, lane, lanes), axis=1, keepdims=True)
    e2 = jnp.exp(m2 - m1)
    denom = 1.0 + e2
    w1 = 1.0 / denom
    w2 = e2 / denom

    onehot = jnp.where((lane == i1) | (lane == i2), 1.0, 0.0)
    row = lax.broadcasted_iota(jnp.int32, (tm, tm), 0)
    col = lax.broadcasted_iota(jnp.int32, (tm, tm), 1)
    earlier = jnp.where(col < row, 1.0, 0.0).astype(BF16)
    within = jnp.dot(earlier, onehot.astype(BF16), preferred_element_type=F32)
    size = jnp.sum(onehot, axis=0, keepdims=True)
    size = jnp.ceil(size * (1.0 / GROUP_ALIGN)) * GROUP_ALIGN
    er = lax.broadcasted_iota(jnp.int32, (lanes, lanes), 0)
    ec = lax.broadcasted_iota(jnp.int32, (lanes, lanes), 1)
    before = jnp.where(er < ec, 1.0, 0.0).astype(BF16)
    size8 = jnp.broadcast_to(size, (8, lanes))
    off8 = jnp.dot(size8.astype(BF16), before, preferred_element_type=F32)
    where_in_tile = within + off8[0:1, :]
    d1 = jnp.sum(jnp.where(lane == i1, where_in_tile, 0.0), axis=1, keepdims=True)
    d2 = jnp.sum(jnp.where(lane == i2, where_in_tile, 0.0), axis=1, keepdims=True)

    meta = jnp.zeros_like(logits)
    for k, val in {META_W1: w1, META_W2: w2, META_D1: d1, META_D2: d2}.items():
        meta = jnp.where(lane == k, val, meta)
    meta_ref[...] = meta
    post_ref[...] = meta.T[:post_ref.shape[0], :].astype(jnp.int32)

    run8 = jnp.broadcast_to(run_ref[...], (8, lanes))
    tab = jnp.where(lane[:8] < TAB_OFF, size8, 0.0)
    tab = tab + pltpu.roll(jnp.where(lane[:8] < n_experts, off8, 0.0), TAB_OFF, 1)
    tab = tab + pltpu.roll(jnp.where(lane[:8] < n_experts, run8, 0.0), TAB_RUN, 1)
    tab_ref[0] = tab
    run_ref[...] += size


def _moe_route(h2, w_router, tm):
    m, d = h2.shape
    n_experts = w_router.shape[1]
    assert n_experts <= MAX_EXPERTS
    w_r = _pad_to(w_router, 1, V7X_LANES)
    lanes = w_r.shape[1]
    blk = 2 * tm * d * 4 + 2 * d * lanes * 4 + 4 * tm * lanes * 4 + tm * tm * 6 + 8 * tm * lanes * 4 + tm * d * 4
    return pl.pallas_call(
        functools.partial(_route_kernel, n_experts=n_experts),
        grid=(m // tm,),
        in_specs=[pl.BlockSpec((tm, d), lambda i: (i, 0)), _const_spec((d, lanes))],
        out_specs=[pl.BlockSpec((tm, lanes), lambda i: (i, 0)),
                   pl.BlockSpec((8, tm), lambda i: (0, i)),
                   pl.BlockSpec((1, 8, lanes), lambda i: (i, 0, 0))],
        out_shape=[jax.ShapeDtypeStruct((m, lanes), F32),
                   jax.ShapeDtypeStruct((8, m), jnp.int32),
                   jax.ShapeDtypeStruct((m // tm, 8, lanes), F32)],
        scratch_shapes=[pltpu.VMEM((1, lanes), F32)],
        compiler_params=pltpu.CompilerParams(
            dimension_semantics=("arbitrary",), vmem_limit_bytes=_vmem_limit(blk)),
        name="moe_route",
    )(h2, w_r)


def _group_copy(src_ref, src_row, dst_ref, dst_row, sem):
    return pltpu.make_async_copy(src_ref.at[pl.ds(src_row, GROUP_ALIGN), :],
                                 dst_ref.at[pl.ds(dst_row, GROUP_ALIGN), :], sem)


def _for_each_group_slab(tab_ref, i, n_experts, fn):
    for e in range(n_experts):
        size, off, dst = tab_ref[i, TAB_SIZE + e], tab_ref[i, TAB_OFF + e], tab_ref[i, TAB_DST + e]

        def slab(k, _):
            step = pl.multiple_of(k * GROUP_ALIGN, GROUP_ALIGN)
            fn(pl.multiple_of(off + step, GROUP_ALIGN), pl.multiple_of(dst + step, GROUP_ALIGN))
            return 0

        lax.fori_loop(0, size // GROUP_ALIGN, slab, 0)


def _split3_bf16(x):
    p0 = x.astype(BF16)
    r = x - p0.astype(F32)
    p1 = r.astype(BF16)
    p2 = (r - p1.astype(F32)).astype(BF16)
    return p0, p1, p2


def _dispatch_kernel(tab_ref, fill_ref, post_ref, h_ref, meta_ref, xs_ref, gs_ref,
                     grouped_ref, gate_ref, zero_ref, zero_gate_ref, sems, *, n_experts):
    i = pl.program_id(0)
    last = pl.num_programs(0) - 1
    buf = i % 2
    rows = grouped_ref.shape[1]
    tm = h_ref.shape[0]
    slot = lax.broadcasted_iota(jnp.int32, (rows, tm), 0)
    sel1 = jnp.where(slot == post_ref[META_D1:META_D1 + 1, :], 1.0, 0.0).astype(BF16)
    sel2 = jnp.where(slot == post_ref[META_D2:META_D2 + 1, :], 1.0, 0.0).astype(BF16)
    grouped_ref[buf] = jnp.dot(sel1 + sel2, h_ref[...].astype(BF16), preferred_element_type=F32).astype(BF16)
    meta = meta_ref[...]
    lane = lax.broadcasted_iota(jnp.int32, meta.shape, 1)
    pieces = []
    for w_lane in (META_W1, META_W2):
        wp = jnp.zeros(meta.shape, F32)
        for k, p in enumerate(_split3_bf16(meta[:, w_lane:w_lane + 1])):
            wp = jnp.where(lane == k, p.astype(F32), wp)
        pieces.append(wp.astype(BF16))
    gate_ref[buf] = (jnp.dot(sel1, pieces[0], preferred_element_type=F32)
                     + jnp.dot(sel2, pieces[1], preferred_element_type=F32))

    def copies(step, b, act):
        def both(src, dst):
            act(_group_copy(grouped_ref.at[b], src, xs_ref, dst, sems.at[b]))
            act(_group_copy(gate_ref.at[b], src, gs_ref, dst, sems.at[b]))
        _for_each_group_slab(tab_ref, step, n_experts, both)

    copies(i, buf, lambda c: c.start())

    @pl.when(i > 0)
    def _():
        copies(i - 1, 1 - buf, lambda c: c.wait())

    @pl.when(i == last)
    def _():
        copies(i, buf, lambda c: c.wait())
        zero_ref[...] = jnp.zeros_like(zero_ref)
        zero_gate_ref[...] = jnp.zeros_like(zero_gate_ref)
        for e in range(fill_ref.shape[1]):
            begin = fill_ref[0, e]
            n = (fill_ref[1, e] - begin) // GROUP_ALIGN

            def fill(k, act):
                dst = pl.multiple_of(begin + k * GROUP_ALIGN, GROUP_ALIGN)
                act(_group_copy(zero_ref, 0, xs_ref, dst, sems.at[0]))
                act(_group_copy(zero_gate_ref, 0, gs_ref, dst, sems.at[0]))
                return 0

            lax.fori_loop(0, n, lambda k, _: fill(k, lambda c: c.start()), 0)
            lax.fori_loop(0, n, lambda k, _: fill(k, lambda c: c.wait()), 0)


def _moe_dispatch(h2, post, meta, tab, fill, rows, n_experts, tm):
    m, d = h2.shape
    lanes = meta.shape[1]
    grouped = _grouped_rows(tm)
    blk = (2 * tm * d * 4 + 2 * 8 * tm * 4 + 2 * tm * lanes * 4 + 2 * grouped * d * 2 + 2 * grouped * lanes * 4
           + grouped * d * 4 + 3 * grouped * tm * 2 + tm * d * 2)
    return pl.pallas_call(
        functools.partial(_dispatch_kernel, n_experts=n_experts),
        grid=(m // tm,),
        in_specs=[pl.BlockSpec(memory_space=pltpu.SMEM),
                  pl.BlockSpec(memory_space=pltpu.SMEM),
                  pl.BlockSpec((8, tm), lambda i: (0, i)),
                  pl.BlockSpec((tm, d), lambda i: (i, 0)),
                  pl.BlockSpec((tm, lanes), lambda i: (i, 0))],
        out_specs=[pl.BlockSpec(memory_space=pl.ANY), pl.BlockSpec(memory_space=pl.ANY)],
        out_shape=[jax.ShapeDtypeStruct((rows, d), BF16), jax.ShapeDtypeStruct((rows, lanes), F32)],
        scratch_shapes=[pltpu.VMEM((2, grouped, d), BF16), pltpu.VMEM((2, grouped, lanes), F32),
                        pltpu.VMEM((GROUP_ALIGN, d), BF16), pltpu.VMEM((GROUP_ALIGN, lanes), F32),
                        pltpu.SemaphoreType.DMA((2,))],
        compiler_params=pltpu.CompilerParams(
            dimension_semantics=("arbitrary",), vmem_limit_bytes=_vmem_limit(blk)),
        name="moe_dispatch",
    )(tab, fill, post, h2, meta)


def _experts_kernel(te_ref, nt_ref, xs_ref, gs_ref, wgu_ref, wd_ref, ys_ref):
    occupied = pl.program_id(0) < nt_ref[0]

    @pl.when(occupied)
    def _():
        fe = wd_ref.shape[1]
        gate = jnp.sum(gs_ref[...], axis=1, keepdims=True)
        y = _swiglu_chunks(xs_ref[...], wgu_ref.at[0], wgu_ref.at[0], wd_ref.at[0], 0, fe, fe)
        ys_ref[...] = (gate * y).astype(ys_ref.dtype)

    @pl.when(jnp.logical_not(occupied))
    def _():
        ys_ref[...] = jnp.zeros_like(ys_ref)


def _moe_experts(xs, gs, w_gu, w_d, tile_expert, n_tiles, tile):
    rows, d = xs.shape
    lanes = gs.shape[1]
    _, _, fe2 = w_gu.shape
    fe = w_d.shape[1]
    blk = 2 * tile * d * 2 * 2 + 2 * d * fe2 * 2 + 2 * fe * d * 2 + tile * fe2 * 4 + tile * fe * 6 + tile * d * 4
    grid_spec = pltpu.PrefetchScalarGridSpec(
        num_scalar_prefetch=2,
        grid=(rows // tile,),
        in_specs=[pl.BlockSpec((tile, d), lambda i, te, nt: (i, 0)),
                  pl.BlockSpec((tile, lanes), lambda i, te, nt: (i, 0)),
                  pl.BlockSpec((1, d, fe2), lambda i, te, nt: (te[i], 0, 0)),
                  pl.BlockSpec((1, fe, d), lambda i, te, nt: (te[i], 0, 0))],
        out_specs=pl.BlockSpec((tile, d), lambda i, te, nt: (i, 0)),
    )
    return pl.pallas_call(
        _experts_kernel,
        grid_spec=grid_spec,
        out_shape=jax.ShapeDtypeStruct((rows, d), BF16),
        compiler_params=pltpu.CompilerParams(
            dimension_semantics=("arbitrary",), vmem_limit_bytes=_vmem_limit(blk)),
        name="moe_experts",
    )(tile_expert, n_tiles, xs, gs, w_gu, w_d)


def _combine_ln_kernel(tab_ref, h_ref, meta_ref, ys_ref, g_ref, b_ref, o_ref, grouped_ref, sems, *, alpha, n_experts):
    i = pl.program_id(0)
    buf = i % 2

    def copies(step, b, act):
        _for_each_group_slab(tab_ref, step, n_experts,
                             lambda row, src: act(_group_copy(ys_ref, src, grouped_ref.at[b], row, sems.at[b])))

    @pl.when(i == 0)
    def _():
        grouped_ref[...] = jnp.zeros_like(grouped_ref)
        copies(0, 0, lambda c: c.start())

    @pl.when(i + 1 < pl.num_programs(0))
    def _():
        copies(i + 1, 1 - buf, lambda c: c.start())

    copies(i, buf, lambda c: c.wait())

    tm = h_ref.shape[0]
    rows = grouped_ref.shape[1]
    meta = meta_ref[...]
    slot = lax.broadcasted_iota(jnp.int32, (tm, rows), 1)
    grouped = grouped_ref[buf]
    d1 = meta[:, META_D1:META_D1 + 1].astype(jnp.int32)
    d2 = meta[:, META_D2:META_D2 + 1].astype(jnp.int32)
    pick = jnp.where((slot == d1) | (slot == d2), 1.0, 0.0).astype(BF16)
    moe = jnp.dot(pick, grouped, preferred_element_type=F32)
    o_ref[...] = _layer_norm(alpha * h_ref[...] + moe, g_ref[...], b_ref[...])


def _moe_combine_ln(h2, tab, meta, ys, g, bvec, alpha, n_experts, tm):
    m, d = h2.shape
    lanes = meta.shape[1]
    grouped = _grouped_rows(tm)
    blk = 2 * tm * d * 4 * 2 + 2 * tm * lanes * 4 + 2 * grouped * d * 2 + 2 * tm * grouped * 2 + 3 * tm * d * 4
    return pl.pallas_call(
        functools.partial(_combine_ln_kernel, alpha=alpha, n_experts=n_experts),
        grid=(m // tm,),
        in_specs=[pl.BlockSpec(memory_space=pltpu.SMEM),
                  pl.BlockSpec((tm, d), lambda i: (i, 0)),
                  pl.BlockSpec((tm, lanes), lambda i: (i, 0)),
                  pl.BlockSpec(memory_space=pl.ANY),
                  _const_spec((1, d)), _const_spec((1, d))],
        out_specs=pl.BlockSpec((tm, d), lambda i: (i, 0)),
        out_shape=jax.ShapeDtypeStruct((m, d), F32),
        scratch_shapes=[pltpu.VMEM((2, grouped, d), BF16), pltpu.SemaphoreType.DMA((2,))],
        compiler_params=pltpu.CompilerParams(
            dimension_semantics=("arbitrary",), vmem_limit_bytes=_vmem_limit(blk)),
        name="moe_combine_ln",
    )(tab, h2, meta, ys, g.reshape(1, d), bvec.reshape(1, d))


def _moe_ln(h2, w_router, w_gu, w_d, g, bvec, alpha, tm, tile):
    m, d = h2.shape
    n_experts = w_router.shape[1]
    n_tok_tiles = m // tm
    meta, post, tab = _moe_route(h2, w_router, tm)

    tab = tab[:, 0, :].astype(jnp.int32)
    sizes = tab[:, TAB_SIZE:TAB_SIZE + n_experts]
    region = jnp.sum(sizes, axis=0)
    n_max = -(-(TOP_K * m + n_tok_tiles * n_experts * (GROUP_ALIGN - 1)) // tile) + n_experts
    rows = n_max * tile
    tiles = (region + (tile - 1)) // tile
    ends = jnp.cumsum(tiles)
    n_tiles = ends[-1]
    start = (ends - tiles) * tile
    tab = tab.at[:, TAB_DST:TAB_DST + n_experts].add(start[None, :])
    fill = jnp.stack([jnp.append(start + region, n_tiles * tile),
                      jnp.append(start + tiles * tile, rows)]).astype(jnp.int32)
    idx = jnp.minimum(jnp.arange(n_max, dtype=jnp.int32), n_tiles - 1)
    tile_expert = jnp.sum(idx[:, None] >= ends[None, :], axis=1).astype(jnp.int32)

    xs, gs = _moe_dispatch(h2, post, meta, tab, fill, rows, n_experts, tm)
    ys = _moe_experts(xs, gs, w_gu, w_d, tile_expert, n_tiles.reshape(1).astype(jnp.int32), tile)
    return _moe_combine_ln(h2, tab, meta, ys, g, bvec, alpha, n_experts, tm)


def kernel(x, even_w_in, even_pool_w, even_pool_scale, even_w_out, even_ln1_g, even_ln1_b, even_ffn_w_gu, even_ffn_w_down, even_ln2_g, even_ln2_b, odd_w_in, odd_sgu_ln_g, odd_sgu_ln_b, odd_sgu_w, odd_sgu_b, odd_w_out, odd_ln1_g, odd_ln1_b, odd_router, odd_moe_w_gu, odd_moe_w_down, odd_ln2_g, odd_ln2_b):
    b, s, d = x.shape
    depth = even_w_in.shape[0] + odd_w_in.shape[0]
    alpha = (2.0 * depth) ** 0.25
    pool_width = even_pool_scale.shape[1]
    sb_width = (even_w_in.shape[2] - pool_width) // 3
    m = b * s
    tm = min(512, m)
    ffn_tm = 512
    moe_tile = 512

    h = x.reshape(m, d)
    for layer in range(depth):
        i = layer // 2
        if layer % 2 == 0:
            p, qkv = _in_proj_even(h, even_w_in[i].astype(BF16), pool_width, tm)
            pooled = _pool_mixer(p.reshape(b, s, pool_width), even_pool_w[i].astype(BF16),
                                 even_pool_scale[i], min(512, s))
            attn = _sb_attention(qkv.reshape(b, s, 3 * sb_width), sb_width)
            h = _out_proj_ln(pooled.reshape(m, pool_width), attn.reshape(m, sb_width), h,
                             even_w_out[i].astype(BF16), even_ln1_g[i], even_ln1_b[i], alpha, tm)
            h = _ffn_ln(h, even_ffn_w_gu[i].astype(BF16), even_ffn_w_down[i].astype(BF16),
                        even_ln2_g[i], even_ln2_b[i], alpha, min(ffn_tm, m), 512)
        else:
            u, v = _in_proj_odd(h, odd_w_in[i].astype(BF16), odd_sgu_ln_g[i], odd_sgu_ln_b[i], tm)
            h = _sgu_out_ln(u, v, odd_sgu_w[i], odd_sgu_b[i], h, odd_w_out[i].astype(BF16),
                            odd_ln1_g[i], odd_ln1_b[i], alpha, tm)
            h = _moe_ln(h, odd_router[i], odd_moe_w_gu[i].astype(BF16), odd_moe_w_down[i].astype(BF16),
                        odd_ln2_g[i], odd_ln2_b[i], alpha, tm, min(moe_tile, m))
    return h.reshape(b, s, d)
```

```python
import functools
import math

import jax
import jax.numpy as jnp
from jax import lax
from jax.experimental import pallas as pl
from jax.experimental.pallas import tpu as pltpu

F32 = jnp.float32
BF16 = jnp.bfloat16

LN_EPS = 1e-5
POOL_WINDOWS = (2, 4, 8, 16)
SB_HEAD_DIM = 128
SGU_GROUPS = 8
SGU_CHUNK = 128
TOP_K = 2

V7X_LANES = 128
V7X_VMEM_BYTES = 64 * 1024 * 1024
BLOCK = 128

F32_EXP_UNDERFLOW = -104.0
NO_WEIGHT = -1e30


def _vmem_limit(block_bytes):
    return int(min(block_bytes * 1.3 + (6 << 20), V7X_VMEM_BYTES - (3 << 20)))


def _layer_norm(r, g, b):
    mu = jnp.mean(r, axis=-1, keepdims=True)
    xc = r - mu
    var = jnp.mean(xc * xc, axis=-1, keepdims=True)
    return xc * lax.rsqrt(var + LN_EPS) * g + b


def _silu(x):
    return x * jax.nn.sigmoid(x)


def _gelu_tanh(x):
    c = math.sqrt(2.0 / math.pi)
    return 0.5 * x * (1.0 + jnp.tanh(c * (x + 0.044715 * (x * x * x))))


def _swiglu_chunks(xb, wg_ref, wu_ref, wd_ref, gate_col, up_col, width, chunk=256):
    y = None
    for lo in range(0, width, chunk):
        n = min(chunk, width - lo)
        gate = jnp.dot(xb, wg_ref[:, gate_col + lo:gate_col + lo + n], preferred_element_type=F32)
        up = jnp.dot(xb, wu_ref[:, up_col + lo:up_col + lo + n], preferred_element_type=F32)
        act = (_silu(gate) * up).astype(BF16)
        part = jnp.dot(act, wd_ref[lo:lo + n, :], preferred_element_type=F32)
        y = part if y is None else y + part
    return y


def _split_bf16(x):
    hi = x.astype(BF16)
    lo = (x - hi.astype(F32)).astype(BF16)
    return hi, lo


def _even_slices(rows, parts=2):
    step = rows // parts
    return [slice(k * step, (k + 1) * step) for k in range(parts)]


def _const_spec(shape):
    nd = len(shape)
    return pl.BlockSpec(shape, lambda *_: (0,) * nd)


def _pad_to(x, axis, mult):
    pad = (-x.shape[axis]) % mult
    if pad == 0:
        return x
    widths = [(0, 0)] * x.ndim
    widths[axis] = (0, pad)
    return jnp.pad(x, widths)


def _in_proj_even_kernel(x_ref, w_ref, p_ref, qkv_ref, *, pool_width):
    xb = x_ref[...].astype(BF16)
    p_ref[...] = jnp.dot(xb, w_ref[:, :pool_width], preferred_element_type=F32)
    n_rest = (w_ref.shape[1] - pool_width) // pool_width
    for c in range(n_rest):
        lo = pool_width * (c + 1)
        qkv_ref[:, c * pool_width:(c + 1) * pool_width] = jnp.dot(
            xb, w_ref[:, lo:lo + pool_width], preferred_element_type=F32).astype(BF16)


def _in_proj_even(x2, w_in, pool_width, tm):
    m, d = x2.shape
    n = w_in.shape[1]
    blk = 2 * tm * d * 4 + 2 * d * n * 2 + 2 * tm * pool_width * 4 + 2 * tm * (n - pool_width) * 2 + tm * n * 4
    return pl.pallas_call(
        functools.partial(_in_proj_even_kernel, pool_width=pool_width),
        grid=(m // tm,),
        in_specs=[pl.BlockSpec((tm, d), lambda i: (i, 0)), _const_spec((d, n))],
        out_specs=[pl.BlockSpec((tm, pool_width), lambda i: (i, 0)),
                   pl.BlockSpec((tm, n - pool_width), lambda i: (i, 0))],
        out_shape=[jax.ShapeDtypeStruct((m, pool_width), F32),
                   jax.ShapeDtypeStruct((m, n - pool_width), BF16)],
        compiler_params=pltpu.CompilerParams(
            dimension_semantics=("arbitrary",), vmem_limit_bytes=_vmem_limit(blk)),
        name="in_proj_even",
    )(x2, w_in)


def _pool_kernel(pc_ref, pp_ref, w_ref, sc_ref, o_ref, *, ts, group_dim):
    i = pl.program_id(1)
    row = lax.broadcasted_iota(jnp.int32, (ts, 1), 0)
    t = i * ts + row
    for g, w in enumerate(POOL_WINDOWS):
        cs = slice(g * group_dim, (g + 1) * group_dim)
        cur = pc_ref[0, :, cs]
        prev = jnp.where(i > 0, pp_ref[0, :, cs], 0.0)
        acc = jnp.concatenate([prev, cur], axis=0)
        k = 1
        while k < w:
            acc = acc + pltpu.roll(acc, k, 0)
            k *= 2
        count = jnp.minimum(t + 1, w).astype(F32)
        d = acc[BLOCK:] / count - cur
        y = jnp.dot(d.astype(BF16), w_ref[g], preferred_element_type=F32) * sc_ref[:, cs]
        o_ref[0, :, cs] = y.astype(o_ref.dtype)


def _pool_mixer(p, pool_w, pool_scale, ts):
    b, s, c = p.shape
    groups = len(POOL_WINDOWS)
    gd = c // groups
    per = ts // BLOCK
    blk = 2 * ts * c * 4 + 2 * BLOCK * c * 4 + 2 * groups * gd * gd * 2 + 2 * ts * c * 2 + 6 * (ts + BLOCK) * gd * 4
    return pl.pallas_call(
        functools.partial(_pool_kernel, ts=ts, group_dim=gd),
        grid=(b, s // ts),
        in_specs=[pl.BlockSpec((1, ts, c), lambda bi, i: (bi, i, 0)),
                  pl.BlockSpec((1, BLOCK, c), lambda bi, i: (bi, jnp.maximum(i * per - 1, 0), 0)),
                  _const_spec((groups, gd, gd)),
                  _const_spec((1, c))],
        out_specs=pl.BlockSpec((1, ts, c), lambda bi, i: (bi, i, 0)),
        out_shape=jax.ShapeDtypeStruct((b, s, c), BF16),
        compiler_params=pltpu.CompilerParams(
            dimension_semantics=("arbitrary", "arbitrary"), vmem_limit_bytes=_vmem_limit(blk)),
        name="pool_mixer",
    )(p, p, pool_w, pool_scale.reshape(1, c))


def _sb_attn_kernel(q_ref, k_ref, v_ref, o_ref, *scratch, n_blocks, group, scale):
    acc_refs, carry_refs = scratch[:group], scratch[group:]
    row = lax.broadcasted_iota(jnp.int32, (BLOCK, BLOCK), 0)
    col = lax.broadcasted_iota(jnp.int32, (BLOCK, BLOCK), 1)
    causal = col < row
    jj = lax.broadcasted_iota(jnp.int32, (BLOCK, 2 * BLOCK), 0)
    ss = lax.broadcasted_iota(jnp.int32, (BLOCK, 2 * BLOCK), 1)
    tmat = jnp.where((jj > ss) | (ss >= BLOCK), 1.0, 0.0).astype(BF16)
    tmat2 = jnp.concatenate([tmat, tmat], axis=0)

    def tiles(qs, ks, vs, carries, mask):
        n = len(qs)
        log_betas, parts = [], []
        for q, kj in zip(qs, ks):
            z = lax.dot_general(q, kj, (((1,), (1,)), ((), ())), preferred_element_type=F32) * scale
            sp = jnp.maximum(z, 0.0) + jnp.log(1.0 + jnp.exp(-jnp.abs(z)))
            log_betas.append(z - sp)
            if mask is not None:
                sp = jnp.where(mask, sp, 0.0)
            parts.append(jnp.concatenate(_split_bf16(sp), axis=1))
        sums = jnp.dot(jnp.concatenate(parts, axis=0), tmat2, preferred_element_type=F32)
        pvs, new_carries = [], []
        for r in range(n):
            s = sums[r * BLOCK:(r + 1) * BLOCK]
            a = jnp.exp(log_betas[r] - s[:, :BLOCK] + carries[r])
            if mask is not None:
                a = jnp.where(mask, a, 0.0)
            pvs.append(jnp.dot(a.astype(BF16), vs[r], preferred_element_type=F32))
            new_carries.append(carries[r] - s[:, BLOCK:])
        return pvs, new_carries

    def top_of(carries):
        m = carries[0]
        for c in carries[1:]:
            m = jnp.maximum(m, c)
        return jnp.max(m)

    def q_group(gi, _):
        base = gi * group
        offs = [pl.multiple_of((base + r) * BLOCK, BLOCK) for r in range(group)]
        load = lambda ref, off: ref[0, pl.ds(off, BLOCK), :]
        pvs, carries = tiles([load(q_ref, o) for o in offs], [load(k_ref, o) for o in offs],
                             [load(v_ref, o) for o in offs], [jnp.zeros((BLOCK, BLOCK), F32)] * group, causal)
        for r in range(group):
            acc_refs[r][...] = pvs[r]
            carry_refs[r][...] = carries[r]

        def cond(c):
            k, top = c
            return jnp.logical_and(k < base + group, top > F32_EXP_UNDERFLOW)

        def body(c):
            k, _ = c
            koffs, carries = [], []
            for r in range(group):
                j = base + r - k
                carries.append(jnp.where(j >= 0, carry_refs[r][...], NO_WEIGHT))
                koffs.append(pl.multiple_of(jnp.maximum(j, 0) * BLOCK, BLOCK))
            pvs, carries = tiles([load(q_ref, o) for o in offs], [load(k_ref, o) for o in koffs],
                                 [load(v_ref, o) for o in koffs], carries, None)
            for r in range(group):
                acc_refs[r][...] += pvs[r]
                carry_refs[r][...] = carries[r]
            return k + 1, top_of(carries)

        lax.while_loop(cond, body, (jnp.int32(1), top_of(carries)))
        for r in range(group):
            o_ref[0, pl.ds(offs[r], BLOCK), :] = acc_refs[r][...].astype(o_ref.dtype)
        return 0

    lax.fori_loop(0, n_blocks // group, q_group, 0)


def _sb_attention(qkv, width):
    b, s, _ = qkv.shape
    heads = width // SB_HEAD_DIM
    n_blocks = s // BLOCK
    group = math.gcd(n_blocks, 32)
    blk = 2 * 4 * s * SB_HEAD_DIM * 2 + 2 * group * BLOCK * BLOCK * 4
    spec = lambda off: pl.BlockSpec((1, s, SB_HEAD_DIM), lambda bi, h: (bi, 0, off * heads + h))
    return pl.pallas_call(
        functools.partial(_sb_attn_kernel, n_blocks=n_blocks, group=group, scale=1.0 / math.sqrt(SB_HEAD_DIM)),
        grid=(b, heads),
        in_specs=[spec(0), spec(1), spec(2)],
        out_specs=pl.BlockSpec((1, s, SB_HEAD_DIM), lambda bi, h: (bi, 0, h)),
        out_shape=jax.ShapeDtypeStruct((b, s, width), BF16),
        scratch_shapes=[pltpu.VMEM((BLOCK, BLOCK), F32)] * (2 * group),
        compiler_params=pltpu.CompilerParams(
            dimension_semantics=("arbitrary", "arbitrary"), vmem_limit_bytes=_vmem_limit(blk)),
        name="sb_attention",
    )(qkv, qkv, qkv)


def _out_proj_ln_kernel(a1_ref, a2_ref, x_ref, w_ref, g_ref, b_ref, o_ref, *, alpha):
    k1 = a1_ref.shape[1]
    for rs in _even_slices(x_ref.shape[0]):
        y = (jnp.dot(a1_ref[rs, :], w_ref[:k1, :], preferred_element_type=F32)
             + jnp.dot(a2_ref[rs, :], w_ref[k1:, :], preferred_element_type=F32))
        o_ref[rs, :] = _layer_norm(alpha * x_ref[rs, :] + y, g_ref[...], b_ref[...])


def _out_proj_ln(a1, a2, x2, w_out, g, bvec, alpha, tm):
    m, d = x2.shape
    k1, k2 = a1.shape[1], a2.shape[1]
    blk = 2 * tm * (k1 + k2) * 2 + 2 * tm * d * 4 + 2 * (k1 + k2) * d * 2 + 2 * tm * d * 4 + tm * d * 4
    return pl.pallas_call(
        functools.partial(_out_proj_ln_kernel, alpha=alpha),
        grid=(m // tm,),
        in_specs=[pl.BlockSpec((tm, k1), lambda i: (i, 0)),
                  pl.BlockSpec((tm, k2), lambda i: (i, 0)),
                  pl.BlockSpec((tm, d), lambda i: (i, 0)),
                  _const_spec((k1 + k2, d)), _const_spec((1, d)), _const_spec((1, d))],
        out_specs=pl.BlockSpec((tm, d), lambda i: (i, 0)),
        out_shape=jax.ShapeDtypeStruct((m, d), F32),
        compiler_params=pltpu.CompilerParams(
            dimension_semantics=("arbitrary",), vmem_limit_bytes=_vmem_limit(blk)),
        name="out_proj_ln_even",
    )(a1, a2, x2, w_out, g.reshape(1, d), bvec.reshape(1, d))


def _ffn_kernel(h_ref, wg_ref, wu_ref, wd_ref, g_ref, b_ref, o_ref, hb_ref, *, alpha):
    j = pl.program_id(1)

    @pl.when(j == 0)
    def _():
        hb_ref[...] = h_ref[...].astype(BF16)
        o_ref[...] = jnp.zeros_like(o_ref)

    o_ref[...] += _swiglu_chunks(hb_ref[...], wg_ref, wu_ref, wd_ref, 0, 0, wd_ref.shape[0])

    @pl.when(j == pl.num_programs(1) - 1)
    def _():
        o_ref[...] = _layer_norm(alpha * h_ref[...] + o_ref[...], g_ref[...], b_ref[...])


def _ffn_ln(h2, w_gu, w_down, g, bvec, alpha, tm, tf):
    m, d = h2.shape
    f = w_down.shape[0]
    nf = f // tf
    blk = 2 * tm * d * 4 + 2 * 2 * d * tf * 2 + 2 * tf * d * 2 + 2 * tm * d * 4 + tm * d * 2 + 3 * tm * tf * 4
    return pl.pallas_call(
        functools.partial(_ffn_kernel, alpha=alpha),
        grid=(m // tm, nf),
        in_specs=[pl.BlockSpec((tm, d), lambda i, j: (i, 0)),
                  pl.BlockSpec((d, tf), lambda i, j: (0, j)),
                  pl.BlockSpec((d, tf), lambda i, j: (0, j + nf)),
                  pl.BlockSpec((tf, d), lambda i, j: (j, 0)),
                  _const_spec((1, d)), _const_spec((1, d))],
        out_specs=pl.BlockSpec((tm, d), lambda i, j: (i, 0)),
        out_shape=jax.ShapeDtypeStruct((m, d), F32),
        scratch_shapes=[pltpu.VMEM((tm, d), BF16)],
        compiler_params=pltpu.CompilerParams(
            dimension_semantics=("arbitrary", "arbitrary"), vmem_limit_bytes=_vmem_limit(blk)),
        name="ffn_ln_even",
    )(h2, w_gu, w_gu, w_down, g.reshape(1, d), bvec.reshape(1, d))


def _in_proj_odd_kernel(h_ref, w_ref, g_ref, b_ref, u_ref, v_ref):
    hb = h_ref[...].astype(BF16)
    width = u_ref.shape[1]
    v = _gelu_tanh(jnp.dot(hb, w_ref[:, width:], preferred_element_type=F32))
    v_ref[...] = _layer_norm(v, g_ref[...], b_ref[...]).astype(BF16)
    for cs in _even_slices(width):
        u_ref[:, cs] = _gelu_tanh(jnp.dot(hb, w_ref[:, cs], preferred_element_type=F32))


def _in_proj_odd(h2, w_in, g, bvec, tm):
    m, d = h2.shape
    n = w_in.shape[1]
    width = n // 2
    blk = 2 * tm * d * 4 + 2 * d * n * 2 + 2 * tm * width * 4 + 2 * tm * width * 2 + 2 * tm * width * 4
    return pl.pallas_call(
        _in_proj_odd_kernel,
        grid=(m // tm,),
        in_specs=[pl.BlockSpec((tm, d), lambda i: (i, 0)), _const_spec((d, n)),
                  _const_spec((1, width)), _const_spec((1, width))],
        out_specs=[pl.BlockSpec((tm, width), lambda i: (i, 0)),
                   pl.BlockSpec((tm, width), lambda i: (i, 0))],
        out_shape=[jax.ShapeDtypeStruct((m, width), F32), jax.ShapeDtypeStruct((m, width), BF16)],
        compiler_params=pltpu.CompilerParams(
            dimension_semantics=("arbitrary",), vmem_limit_bytes=_vmem_limit(blk)),
        name="in_proj_odd",
    )(h2, w_in, g.reshape(1, width), bvec.reshape(1, width))


def _sgu_out_ln_kernel(u_ref, v_ref, ws_ref, bs_ref, h_ref, w_ref, g_ref, b_ref, o_ref, gated_ref, *, alpha):
    tm, width = u_ref.shape
    gd = width // SGU_GROUPS
    row = lax.broadcasted_iota(jnp.int32, (SGU_CHUNK, SGU_CHUNK), 0)
    col = lax.broadcasted_iota(jnp.int32, (SGU_CHUNK, SGU_CHUNK), 1)
    lower = col <= row
    for gi in range(SGU_GROUPS):
        ws = jnp.where(lower, ws_ref[gi], 0.0).astype(BF16)
        bias = bs_ref[gi]
        cs = slice(gi * gd, (gi + 1) * gd)
        for n in range(tm // SGU_CHUNK):
            rs = slice(n * SGU_CHUNK, (n + 1) * SGU_CHUNK)
            mixed = jnp.dot(ws, v_ref[rs, cs], preferred_element_type=F32) + bias
            gated_ref[rs, cs] = (u_ref[rs, cs] * mixed).astype(BF16)
    for rs in _even_slices(tm):
        y = jnp.dot(gated_ref[rs, :], w_ref[...], preferred_element_type=F32)
        o_ref[rs, :] = _layer_norm(alpha * h_ref[rs, :] + y, g_ref[...], b_ref[...])


def _sgu_out_ln(u, v, w_s, b_s, h2, w_out, g, bvec, alpha, tm):
    m, d = h2.shape
    width = u.shape[1]
    blk = (2 * tm * width * 4 + 2 * tm * width * 2 + 2 * tm * d * 4 + 2 * width * d * 2 + 2 * tm * d * 4
           + tm * width * 2 + tm * d * 4 + 4 * SGU_GROUPS * SGU_CHUNK * V7X_LANES * 4)
    return pl.pallas_call(
        functools.partial(_sgu_out_ln_kernel, alpha=alpha),
        grid=(m // tm,),
        in_specs=[pl.BlockSpec((tm, width), lambda i: (i, 0)),
                  pl.BlockSpec((tm, width), lambda i: (i, 0)),
                  _const_spec((SGU_GROUPS, SGU_CHUNK, SGU_CHUNK)),
                  _const_spec((SGU_GROUPS, SGU_CHUNK, 1)),
                  pl.BlockSpec((tm, d), lambda i: (i, 0)),
                  _const_spec((width, d)), _const_spec((1, d)), _const_spec((1, d))],
        out_specs=pl.BlockSpec((tm, d), lambda i: (i, 0)),
        out_shape=jax.ShapeDtypeStruct((m, d), F32),
        scratch_shapes=[pltpu.VMEM((tm, width), BF16)],
        compiler_params=pltpu.CompilerParams(
            dimension_semantics=("arbitrary",), vmem_limit_bytes=_vmem_limit(blk)),
        name="sgu_out_ln_odd",
    )(u, v, w_s, b_s.reshape(SGU_GROUPS, SGU_CHUNK, 1), h2, w_out, g.reshape(1, d), bvec.reshape(1, d))


META_W1, META_W2, META_D1, META_D2 = range(4)
TAB_SIZE, TAB_OFF, TAB_RUN = 0, 8, 16
TAB_DST = TAB_RUN
GROUP_ALIGN = 16
MAX_EXPERTS = 8


def _grouped_rows(tm):
    return -(-(TOP_K * tm + MAX_EXPERTS * (GROUP_ALIGN - 1)) // V7X_LANES) * V7X_LANES


def _route_kernel(h_ref, wr_ref, meta_ref, post_ref, tab_ref, run_ref, *, n_experts):
    @pl.when(pl.program_id(0) == 0)
    def _():
        run_ref[...] = jnp.zeros_like(run_ref)

    tm = h_ref.shape[0]
    lanes = wr_ref.shape[1]
    parts = jnp.dot(jnp.concatenate(_split_bf16(h_ref[...]), axis=0),
                    jnp.concatenate(_split_bf16(wr_ref[...]), axis=1), preferred_element_type=F32)
    logits = (parts[:tm, :lanes] + parts[:tm, lanes:]) + (parts[tm:, :lanes] + parts[tm:, lanes:])
    lane = lax.broadcasted_iota(jnp.int32, logits.shape, 1)
    neg = jnp.float32(-jnp.inf)
    lg = jnp.where(lane < n_experts, logits, neg)
    m1 = jnp.max(lg, axis=1, keepdims=True)
    i1 = jnp.min(jnp.where(lg == m1, lane, lanes), axis=1, keepdims=True)
    lg2 = jnp.where(lane == i1, neg, lg)
    m2 = jnp.max(lg2, axis=1, keepdims=True)
    i2 = jnp.min(jnp.where(lg2 == m2, lane, lanes), axis=1, keepdims=True)
    e2 = jnp.exp(m2 - m1)
    denom = 1.0 + e2
    w1 = 1.0 / denom
    w2 = e2 / denom

    onehot = jnp.where((lane == i1) | (lane == i2), 1.0, 0.0)
    row = lax.broadcasted_iota(jnp.int32, (tm, tm), 0)
    col = lax.broadcasted_iota(jnp.int32, (tm, tm), 1)
    earlier = jnp.where(col < row, 1.0, 0.0).astype(BF16)
    within = jnp.dot(earlier, onehot.astype(BF16), preferred_element_type=F32)
    size = jnp.sum(onehot, axis=0, keepdims=True)
    size = jnp.ceil(size * (1.0 / GROUP_ALIGN)) * GROUP_ALIGN
    er = lax.broadcasted_iota(jnp.int32, (lanes, lanes), 0)
    ec = lax.broadcasted_iota(jnp.int32, (lanes, lanes), 1)
    before = jnp.where(er < ec, 1.0, 0.0).astype(BF16)
    size8 = jnp.broadcast_to(size, (8, lanes))
    off8 = jnp.dot(size8.astype(BF16), before, preferred_element_type=F32)
    where_in_tile = within + off8[0:1, :]
    d1 = jnp.sum(jnp.where(lane == i1, where_in_tile, 0.0), axis=1, keepdims=True)
    d2 = jnp.sum(jnp.where(lane == i2, where_in_tile, 0.0), axis=1, keepdims=True)

    meta = jnp.zeros_like(logits)
    for k, val in {META_W1: w1, META_W2: w2, META_D1: d1, META_D2: d2}.items():
        meta = jnp.where(lane == k, val, meta)
    meta_ref[...] = meta
    post_ref[...] = meta.T[:post_ref.shape[0], :].astype(jnp.int32)

    run8 = jnp.broadcast_to(run_ref[...], (8, lanes))
    tab = jnp.where(lane[:8] < TAB_OFF, size8, 0.0)
    tab = tab + pltpu.roll(jnp.where(lane[:8] < n_experts, off8, 0.0), TAB_OFF, 1)
    tab = tab + pltpu.roll(jnp.where(lane[:8] < n_experts, run8, 0.0), TAB_RUN, 1)
    tab_ref[0] = tab
    run_ref[...] += size


def _moe_route(h2, w_router, tm):
    m, d = h2.shape
    n_experts = w_router.shape[1]
    assert n_experts <= MAX_EXPERTS
    w_r = _pad_to(w_router, 1, V7X_LANES)
    lanes = w_r.shape[1]
    blk = 2 * tm * d * 4 + 2 * d * lanes * 4 + 4 * tm * lanes * 4 + tm * tm * 6 + 8 * tm * lanes * 4 + tm * d * 4
    return pl.pallas_call(
        functools.partial(_route_kernel, n_experts=n_experts),
        grid=(m // tm,),
        in_specs=[pl.BlockSpec((tm, d), lambda i: (i, 0)), _const_spec((d, lanes))],
        out_specs=[pl.BlockSpec((tm, lanes), lambda i: (i, 0)),
                   pl.BlockSpec((8, tm), lambda i: (0, i)),
                   pl.BlockSpec((1, 8, lanes), lambda i: (i, 0, 0))],
        out_shape=[jax.ShapeDtypeStruct((m, lanes), F32),
                   jax.ShapeDtypeStruct((8, m), jnp.int32),
                   jax.ShapeDtypeStruct((m // tm, 8, lanes), F32)],
        scratch_shapes=[pltpu.VMEM((1, lanes), F32)],
        compiler_params=pltpu.CompilerParams(
            dimension_semantics=("arbitrary",), vmem_limit_bytes=_vmem_limit(blk)),
        name="moe_route",
    )(h2, w_r)


def _group_copy(src_ref, src_row, dst_ref, dst_row, sem):
    return pltpu.make_async_copy(src_ref.at[pl.ds(src_row, GROUP_ALIGN), :],
                                 dst_ref.at[pl.ds(dst_row, GROUP_ALIGN), :], sem)


def _for_each_group_slab(tab_ref, i, n_experts, fn):
    for e in range(n_experts):
        size, off, dst = tab_ref[i, TAB_SIZE + e], tab_ref[i, TAB_OFF + e], tab_ref[i, TAB_DST + e]

        def slab(k, _):
            step = pl.multiple_of(k * GROUP_ALIGN, GROUP_ALIGN)
            fn(pl.multiple_of(off + step, GROUP_ALIGN), pl.multiple_of(dst + step, GROUP_ALIGN))
            return 0

        lax.fori_loop(0, size // GROUP_ALIGN, slab, 0)


def _split3_bf16(x):
    p0 = x.astype(BF16)
    r = x - p0.astype(F32)
    p1 = r.astype(BF16)
    p2 = (r - p1.astype(F32)).astype(BF16)
    return p0, p1, p2


def _dispatch_kernel(tab_ref, fill_ref, post_ref, h_ref, meta_ref, xs_ref, gs_ref,
                     grouped_ref, gate_ref, zero_ref, zero_gate_ref, sems, *, n_experts):
    i = pl.program_id(0)
    last = pl.num_programs(0) - 1
    buf = i % 2
    rows = grouped_ref.shape[1]
    tm = h_ref.shape[0]
    slot = lax.broadcasted_iota(jnp.int32, (rows, tm), 0)
    sel1 = jnp.where(slot == post_ref[META_D1:META_D1 + 1, :], 1.0, 0.0).astype(BF16)
    sel2 = jnp.where(slot == post_ref[META_D2:META_D2 + 1, :], 1.0, 0.0).astype(BF16)
    grouped_ref[buf] = jnp.dot(sel1 + sel2, h_ref[...].astype(BF16), preferred_element_type=F32).astype(BF16)
    meta = meta_ref[...]
    lane = lax.broadcasted_iota(jnp.int32, meta.shape, 1)
    pieces = []
    for w_lane in (META_W1, META_W2):
        wp = jnp.zeros(meta.shape, F32)
        for k, p in enumerate(_split3_bf16(meta[:, w_lane:w_lane + 1])):
            wp = jnp.where(lane == k, p.astype(F32), wp)
        pieces.append(wp.astype(BF16))
    gate_ref[buf] = (jnp.dot(sel1, pieces[0], preferred_element_type=F32)
                     + jnp.dot(sel2, pieces[1], preferred_element_type=F32))

    def copies(step, b, act):
        def both(src, dst):
            act(_group_copy(grouped_ref.at[b], src, xs_ref, dst, sems.at[b]))
            act(_group_copy(gate_ref.at[b], src, gs_ref, dst, sems.at[b]))
        _for_each_group_slab(tab_ref, step, n_experts, both)

    copies(i, buf, lambda c: c.start())

    @pl.when(i > 0)
    def _():
        copies(i - 1, 1 - buf, lambda c: c.wait())

    @pl.when(i == last)
    def _():
        copies(i, buf, lambda c: c.wait())
        zero_ref[...] = jnp.zeros_like(zero_ref)
        zero_gate_ref[...] = jnp.zeros_like(zero_gate_ref)
        for e in range(fill_ref.shape[1]):
            begin = fill_ref[0, e]
            n = (fill_ref[1, e] - begin) // GROUP_ALIGN

            def fill(k, act):
                dst = pl.multiple_of(begin + k * GROUP_ALIGN, GROUP_ALIGN)
                act(_group_copy(zero_ref, 0, xs_ref, dst, sems.at[0]))
                act(_group_copy(zero_gate_ref, 0, gs_ref, dst, sems.at[0]))
                return 0

            lax.fori_loop(0, n, lambda k, _: fill(k, lambda c: c.start()), 0)
            lax.fori_loop(0, n, lambda k, _: fill(k, lambda c: c.wait()), 0)


def _moe_dispatch(h2, post, meta, tab, fill, rows, n_experts, tm):
    m, d = h2.shape
    lanes = meta.shape[1]
    grouped = _grouped_rows(tm)
    blk = (2 * tm * d * 4 + 2 * 8 * tm * 4 + 2 * tm * lanes * 4 + 2 * grouped * d * 2 + 2 * grouped * lanes * 4
           + grouped * d * 4 + 3 * grouped * tm * 2 + tm * d * 2)
    return pl.pallas_call(
        functools.partial(_dispatch_kernel, n_experts=n_experts),
        grid=(m // tm,),
        in_specs=[pl.BlockSpec(memory_space=pltpu.SMEM),
                  pl.BlockSpec(memory_space=pltpu.SMEM),
                  pl.BlockSpec((8, tm), lambda i: (0, i)),
                  pl.BlockSpec((tm, d), lambda i: (i, 0)),
                  pl.BlockSpec((tm, lanes), lambda i: (i, 0))],
        out_specs=[pl.BlockSpec(memory_space=pl.ANY), pl.BlockSpec(memory_space=pl.ANY)],
        out_shape=[jax.ShapeDtypeStruct((rows, d), BF16), jax.ShapeDtypeStruct((rows, lanes), F32)],
        scratch_shapes=[pltpu.VMEM((2, grouped, d), BF16), pltpu.VMEM((2, grouped, lanes), F32),
                        pltpu.VMEM((GROUP_ALIGN, d), BF16), pltpu.VMEM((GROUP_ALIGN, lanes), F32),
                        pltpu.SemaphoreType.DMA((2,))],
        compiler_params=pltpu.CompilerParams(
            dimension_semantics=("arbitrary",), vmem_limit_bytes=_vmem_limit(blk)),
        name="moe_dispatch",
    )(tab, fill, post, h2, meta)


def _experts_kernel(te_ref, nt_ref, xs_ref, gs_ref, wgu_ref, wd_ref, ys_ref):
    occupied = pl.program_id(0) < nt_ref[0]

    @pl.when(occupied)
    def _():
        fe = wd_ref.shape[1]
        gate = jnp.sum(gs_ref[...], axis=1, keepdims=True)
        y = _swiglu_chunks(xs_ref[...], wgu_ref.at[0], wgu_ref.at[0], wd_ref.at[0], 0, fe, fe)
        ys_ref[...] = (gate * y).astype(ys_ref.dtype)

    @pl.when(jnp.logical_not(occupied))
    def _():
        ys_ref[...] = jnp.zeros_like(ys_ref)


def _moe_experts(xs, gs, w_gu, w_d, tile_expert, n_tiles, tile):
    rows, d = xs.shape
    lanes = gs.shape[1]
    _, _, fe2 = w_gu.shape
    fe = w_d.shape[1]
    blk = 2 * tile * d * 2 * 2 + 2 * d * fe2 * 2 + 2 * fe * d * 2 + tile * fe2 * 4 + tile * fe * 6 + tile * d * 4
    grid_spec = pltpu.PrefetchScalarGridSpec(
        num_scalar_prefetch=2,
        grid=(rows // tile,),
        in_specs=[pl.BlockSpec((tile, d), lambda i, te, nt: (i, 0)),
                  pl.BlockSpec((tile, lanes), lambda i, te, nt: (i, 0)),
                  pl.BlockSpec((1, d, fe2), lambda i, te, nt: (te[i], 0, 0)),
                  pl.BlockSpec((1, fe, d), lambda i, te, nt: (te[i], 0, 0))],
        out_specs=pl.BlockSpec((tile, d), lambda i, te, nt: (i, 0)),
    )
    return pl.pallas_call(
        _experts_kernel,
        grid_spec=grid_spec,
        out_shape=jax.ShapeDtypeStruct((rows, d), BF16),
        compiler_params=pltpu.CompilerParams(
            dimension_semantics=("arbitrary",), vmem_limit_bytes=_vmem_limit(blk)),
        name="moe_experts",
    )(tile_expert, n_tiles, xs, gs, w_gu, w_d)


def _combine_ln_kernel(tab_ref, h_ref, meta_ref, ys_ref, g_ref, b_ref, o_ref, grouped_ref, sems, *, alpha, n_experts):
    i = pl.program_id(0)
    buf = i % 2

    def copies(step, b, act):
        _for_each_group_slab(tab_ref, step, n_experts,
                             lambda row, src: act(_group_copy(ys_ref, src, grouped_ref.at[b], row, sems.at[b])))

    @pl.when(i == 0)
    def _():
        grouped_ref[...] = jnp.zeros_like(grouped_ref)
        copies(0, 0, lambda c: c.start())

    @pl.when(i + 1 < pl.num_programs(0))
    def _():
        copies(i + 1, 1 - buf, lambda c: c.start())

    copies(i, buf, lambda c: c.wait())

    tm = h_ref.shape[0]
    rows = grouped_ref.shape[1]
    meta = meta_ref[...]
    slot = lax.broadcasted_iota(jnp.int32, (tm, rows), 1)
    grouped = grouped_ref[buf]
    d1 = meta[:, META_D1:META_D1 + 1].astype(jnp.int32)
    d2 = meta[:, META_D2:META_D2 + 1].astype(jnp.int32)
    pick = jnp.where((slot == d1) | (slot == d2), 1.0, 0.0).astype(BF16)
    moe = jnp.dot(pick, grouped, preferred_element_type=F32)
    o_ref[...] = _layer_norm(alpha * h_ref[...] + moe, g_ref[...], b_ref[...])


def _moe_combine_ln(h2, tab, meta, ys, g, bvec, alpha, n_experts, tm):
    m, d = h2.shape
    lanes = meta.shape[1]
    grouped = _grouped_rows(tm)
    blk = 2 * tm * d * 4 * 2 + 2 * tm * lanes * 4 + 2 * grouped * d * 2 + 2 * tm * grouped * 2 + 3 * tm * d * 4
    return pl.pallas_call(
        functools.partial(_combine_ln_kernel, alpha=alpha, n_experts=n_experts),
        grid=(m // tm,),
        in_specs=[pl.BlockSpec(memory_space=pltpu.SMEM),
                  pl.BlockSpec((tm, d), lambda i: (i, 0)),
                  pl.BlockSpec((tm, lanes), lambda i: (i, 0)),
                  pl.BlockSpec(memory_space=pl.ANY),
                  _const_spec((1, d)), _const_spec((1, d))],
        out_specs=pl.BlockSpec((tm, d), lambda i: (i, 0)),
        out_shape=jax.ShapeDtypeStruct((m, d), F32),
        scratch_shapes=[pltpu.VMEM((2, grouped, d), BF16), pltpu.SemaphoreType.DMA((2,))],
        compiler_params=pltpu.CompilerParams(
            dimension_semantics=("arbitrary",), vmem_limit_bytes=_vmem_limit(blk)),
        name="moe_combine_ln",
    )(tab, h2, meta, ys, g.reshape(1, d), bvec.reshape(1, d))


def _moe_ln(h2, w_router, w_gu, w_d, g, bvec, alpha, tm, tile):
    m, d = h2.shape
    n_experts = w_router.shape[1]
    n_tok_tiles = m // tm
    meta, post, tab = _moe_route(h2, w_router, tm)

    tab = tab[:, 0, :].astype(jnp.int32)
    sizes = tab[:, TAB_SIZE:TAB_SIZE + n_experts]
    region = jnp.sum(sizes, axis=0)
    n_max = -(-(TOP_K * m + n_tok_tiles * n_experts * (GROUP_ALIGN - 1)) // tile) + n_experts
    rows = n_max * tile
    tiles = (region + (tile - 1)) // tile
    ends = jnp.cumsum(tiles)
    n_tiles = ends[-1]
    start = (ends - tiles) * tile
    tab = tab.at[:, TAB_DST:TAB_DST + n_experts].add(start[None, :])
    fill = jnp.stack([jnp.append(start + region, n_tiles * tile),
                      jnp.append(start + tiles * tile, rows)]).astype(jnp.int32)
    idx = jnp.minimum(jnp.arange(n_max, dtype=jnp.int32), n_tiles - 1)
    tile_expert = jnp.sum(idx[:, None] >= ends[None, :], axis=1).astype(jnp.int32)

    xs, gs = _moe_dispatch(h2, post, meta, tab, fill, rows, n_experts, tm)
    ys = _moe_experts(xs, gs, w_gu, w_d, tile_expert, n_tiles.reshape(1).astype(jnp.int32), tile)
    return _moe_combine_ln(h2, tab, meta, ys, g, bvec, alpha, n_experts, tm)


def kernel(x, even_w_in, even_pool_w, even_pool_scale, even_w_out, even_ln1_g, even_ln1_b, even_ffn_w_gu, even_ffn_w_down, even_ln2_g, even_ln2_b, odd_w_in, odd_sgu_ln_g, odd_sgu_ln_b, odd_sgu_w, odd_sgu_b, odd_w_out, odd_ln1_g, odd_ln1_b, odd_router, odd_moe_w_gu, odd_moe_w_down, odd_ln2_g, odd_ln2_b):
    b, s, d = x.shape
    depth = even_w_in.shape[0] + odd_w_in.shape[0]
    alpha = (2.0 * depth) ** 0.25
    pool_width = even_pool_scale.shape[1]
    sb_width = (even_w_in.shape[2] - pool_width) // 3
    m = b * s
    tm = min(512, m)
    ffn_tm = 512
    moe_tile = 512

    h = x.reshape(m, d)
    for layer in range(depth):
        i = layer // 2
        if layer % 2 == 0:
            p, qkv = _in_proj_even(h, even_w_in[i].astype(BF16), pool_width, tm)
            pooled = _pool_mixer(p.reshape(b, s, pool_width), even_pool_w[i].astype(BF16),
                                 even_pool_scale[i], min(512, s))
            attn = _sb_attention(qkv.reshape(b, s, 3 * sb_width), sb_width)
            h = _out_proj_ln(pooled.reshape(m, pool_width), attn.reshape(m, sb_width), h,
                             even_w_out[i].astype(BF16), even_ln1_g[i], even_ln1_b[i], alpha, tm)
            h = _ffn_ln(h, even_ffn_w_gu[i].astype(BF16), even_ffn_w_down[i].astype(BF16),
                        even_ln2_g[i], even_ln2_b[i], alpha, min(ffn_tm, m), 512)
        else:
            u, v = _in_proj_odd(h, odd_w_in[i].astype(BF16), odd_sgu_ln_g[i], odd_sgu_ln_b[i], tm)
            h = _sgu_out_ln(u, v, odd_sgu_w[i], odd_sgu_b[i], h, odd_w_out[i].astype(BF16),
                            odd_ln1_g[i], odd_ln1_b[i], alpha, tm)
            h = _moe_ln(h, odd_router[i], odd_moe_w_gu[i].astype(BF16), odd_moe_w_down[i].astype(BF16),
                        odd_ln2_g[i], odd_ln2_b[i], alpha, tm, min(moe_tile, m))
    return h.reshape(b, s, d)
```

```python
import functools
import math

import jax
import jax.numpy as jnp
from jax import lax
from jax.experimental import pallas as pl
from jax.experimental.pallas import tpu as pltpu

F32 = jnp.float32
BF16 = jnp.bfloat16

LN_EPS = 1e-5
POOL_WINDOWS = (2, 4, 8, 16)
SB_HEAD_DIM = 128
SGU_GROUPS = 8
SGU_CHUNK = 128
TOP_K = 2

V7X_LANES = 128
V7X_MXU_DIM = 256
V7X_VMEM_BYTES = 64 * 1024 * 1024
BLOCK = 128

ROW_TILE = 512
POOL_TILE = 1024
FFN_CHUNK = 512
EXPERT_TILE = 512
ATTN_GROUP = 64

F32_EXP_UNDERFLOW = -104.0
NO_WEIGHT = -1e30


def _vmem_limit(block_bytes):
    return int(min(block_bytes * 1.3 + (6 << 20), V7X_VMEM_BYTES - (3 << 20)))


def _layer_norm(r, g, b):
    mu = jnp.mean(r, axis=-1, keepdims=True)
    xc = r - mu
    var = jnp.mean(xc * xc, axis=-1, keepdims=True)
    return xc * lax.rsqrt(var + LN_EPS) * g + b


def _silu(x):
    return x * jax.nn.sigmoid(x)


def _gelu_tanh(x):
    c = math.sqrt(2.0 / math.pi)
    return 0.5 * x * (1.0 + jnp.tanh(c * (x + 0.044715 * (x * x * x))))


def _swiglu_chunks(xb, wg_ref, wu_ref, wd_ref, gate_col, up_col, width, chunk=V7X_MXU_DIM):
    y = None
    for lo in range(0, width, chunk):
        n = min(chunk, width - lo)
        gate = jnp.dot(xb, wg_ref[:, gate_col + lo:gate_col + lo + n], preferred_element_type=F32)
        up = jnp.dot(xb, wu_ref[:, up_col + lo:up_col + lo + n], preferred_element_type=F32)
        act = (_silu(gate) * up).astype(BF16)
        part = jnp.dot(act, wd_ref[lo:lo + n, :], preferred_element_type=F32)
        y = part if y is None else y + part
    return y


def _split_bf16(x):
    hi = x.astype(BF16)
    lo = (x - hi.astype(F32)).astype(BF16)
    return hi, lo


def _even_slices(rows, parts=2):
    step = rows // parts
    return [slice(k * step, (k + 1) * step) for k in range(parts)]


def _const_spec(shape):
    nd = len(shape)
    return pl.BlockSpec(shape, lambda *_: (0,) * nd)


def _pad_to(x, axis, mult):
    pad = (-x.shape[axis]) % mult
    if pad == 0:
        return x
    widths = [(0, 0)] * x.ndim
    widths[axis] = (0, pad)
    return jnp.pad(x, widths)


def _in_proj_even_kernel(x_ref, w_ref, p_ref, qkv_ref, *, pool_width):
    xb = x_ref[...].astype(BF16)
    p_ref[...] = jnp.dot(xb, w_ref[:, :pool_width], preferred_element_type=F32)
    n_rest = (w_ref.shape[1] - pool_width) // pool_width
    for c in range(n_rest):
        lo = pool_width * (c + 1)
        qkv_ref[:, c * pool_width:(c + 1) * pool_width] = jnp.dot(
            xb, w_ref[:, lo:lo + pool_width], preferred_element_type=F32).astype(BF16)


def _in_proj_even(x2, w_in, pool_width, tm):
    m, d = x2.shape
    n = w_in.shape[1]
    blk = 2 * tm * d * 4 + 2 * d * n * 2 + 2 * tm * pool_width * 4 + 2 * tm * (n - pool_width) * 2 + tm * n * 4
    return pl.pallas_call(
        functools.partial(_in_proj_even_kernel, pool_width=pool_width),
        grid=(m // tm,),
        in_specs=[pl.BlockSpec((tm, d), lambda i: (i, 0)), _const_spec((d, n))],
        out_specs=[pl.BlockSpec((tm, pool_width), lambda i: (i, 0)),
                   pl.BlockSpec((tm, n - pool_width), lambda i: (i, 0))],
        out_shape=[jax.ShapeDtypeStruct((m, pool_width), F32),
                   jax.ShapeDtypeStruct((m, n - pool_width), BF16)],
        compiler_params=pltpu.CompilerParams(
            dimension_semantics=("arbitrary",), vmem_limit_bytes=_vmem_limit(blk)),
        name="in_proj_even",
    )(x2, w_in)


def _pool_kernel(pc_ref, pp_ref, w_ref, sc_ref, o_ref, *, ts, group_dim):
    i = pl.program_id(1)
    row = lax.broadcasted_iota(jnp.int32, (ts, 1), 0)
    t = i * ts + row
    for g, w in enumerate(POOL_WINDOWS):
        cs = slice(g * group_dim, (g + 1) * group_dim)
        cur = pc_ref[0, :, cs]
        prev = jnp.where(i > 0, pp_ref[0, :, cs], 0.0)
        acc = jnp.concatenate([prev, cur], axis=0)
        k = 1
        while k < w:
            acc = acc + pltpu.roll(acc, k, 0)
            k *= 2
        count = jnp.minimum(t + 1, w).astype(F32)
        d = acc[BLOCK:] / count - cur
        y = jnp.dot(d.astype(BF16), w_ref[g], preferred_element_type=F32) * sc_ref[:, cs]
        o_ref[0, :, cs] = y.astype(o_ref.dtype)


def _pool_mixer(p, pool_w, pool_scale, ts):
    b, s, c = p.shape
    groups = len(POOL_WINDOWS)
    gd = c // groups
    per = ts // BLOCK
    blk = 2 * ts * c * 4 + 2 * BLOCK * c * 4 + 2 * groups * gd * gd * 2 + 2 * ts * c * 2 + 6 * (ts + BLOCK) * gd * 4
    return pl.pallas_call(
        functools.partial(_pool_kernel, ts=ts, group_dim=gd),
        grid=(b, s // ts),
        in_specs=[pl.BlockSpec((1, ts, c), lambda bi, i: (bi, i, 0)),
                  pl.BlockSpec((1, BLOCK, c), lambda bi, i: (bi, jnp.maximum(i * per - 1, 0), 0)),
                  _const_spec((groups, gd, gd)),
                  _const_spec((1, c))],
        out_specs=pl.BlockSpec((1, ts, c), lambda bi, i: (bi, i, 0)),
        out_shape=jax.ShapeDtypeStruct((b, s, c), BF16),
        compiler_params=pltpu.CompilerParams(
            dimension_semantics=("arbitrary", "arbitrary"), vmem_limit_bytes=_vmem_limit(blk)),
        name="pool_mixer",
    )(p, p, pool_w, pool_scale.reshape(1, c))


def _sb_attn_kernel(q_ref, k_ref, v_ref, o_ref, *scratch, n_blocks, group, scale):
    acc_refs, carry_refs = scratch[:group], scratch[group:]
    row = lax.broadcasted_iota(jnp.int32, (BLOCK, BLOCK), 0)
    col = lax.broadcasted_iota(jnp.int32, (BLOCK, BLOCK), 1)
    causal = col < row
    jj = lax.broadcasted_iota(jnp.int32, (BLOCK, 2 * BLOCK), 0)
    ss = lax.broadcasted_iota(jnp.int32, (BLOCK, 2 * BLOCK), 1)
    tmat = jnp.where((jj > ss) | (ss >= BLOCK), 1.0, 0.0).astype(BF16)
    tmat2 = jnp.concatenate([tmat, tmat], axis=0)

    def tiles(qs, ks, vs, carries, mask):
        n = len(qs)
        log_betas, parts = [], []
        for q, kj in zip(qs, ks):
            z = lax.dot_general(q, kj, (((1,), (1,)), ((), ())), preferred_element_type=F32) * scale
            sp = jnp.maximum(z, 0.0) + jnp.log(1.0 + jnp.exp(-jnp.abs(z)))
            log_betas.append(z - sp)
            if mask is not None:
                sp = jnp.where(mask, sp, 0.0)
            parts.append(jnp.concatenate(_split_bf16(sp), axis=1))
        sums = jnp.dot(jnp.concatenate(parts, axis=0), tmat2, preferred_element_type=F32)
        pvs, new_carries = [], []
        for r in range(n):
            s = sums[r * BLOCK:(r + 1) * BLOCK]
            a = jnp.exp(log_betas[r] - s[:, :BLOCK] + carries[r])
            if mask is not None:
                a = jnp.where(mask, a, 0.0)
            pvs.append(jnp.dot(a.astype(BF16), vs[r], preferred_element_type=F32))
            new_carries.append(carries[r] - s[:, BLOCK:])
        return pvs, new_carries

    def top_of(carries):
        m = carries[0]
        for c in carries[1:]:
            m = jnp.maximum(m, c)
        return jnp.max(m)

    def q_group(gi, _):
        base = gi * group
        offs = [pl.multiple_of((base + r) * BLOCK, BLOCK) for r in range(group)]
        load = lambda ref, off: ref[0, pl.ds(off, BLOCK), :]
        pvs, carries = tiles([load(q_ref, o) for o in offs], [load(k_ref, o) for o in offs],
                             [load(v_ref, o) for o in offs], [jnp.zeros((BLOCK, BLOCK), F32)] * group, causal)
        for r in range(group):
            acc_refs[r][...] = pvs[r]
            carry_refs[r][...] = carries[r]

        def cond(c):
            k, top = c
            return jnp.logical_and(k < base + group, top > F32_EXP_UNDERFLOW)

        def body(c):
            k, _ = c
            koffs, carries = [], []
            for r in range(group):
                j = base + r - k
                carries.append(jnp.where(j >= 0, carry_refs[r][...], NO_WEIGHT))
                koffs.append(pl.multiple_of(jnp.maximum(j, 0) * BLOCK, BLOCK))
            pvs, carries = tiles([load(q_ref, o) for o in offs], [load(k_ref, o) for o in koffs],
                                 [load(v_ref, o) for o in koffs], carries, None)
            for r in range(group):
                acc_refs[r][...] += pvs[r]
                carry_refs[r][...] = carries[r]
            return k + 1, top_of(carries)

        lax.while_loop(cond, body, (jnp.int32(1), top_of(carries)))
        for r in range(group):
            o_ref[0, pl.ds(offs[r], BLOCK), :] = acc_refs[r][...].astype(o_ref.dtype)
        return 0

    lax.fori_loop(0, n_blocks // group, q_group, 0)


def _sb_attention(qkv, width):
    b, s, _ = qkv.shape
    heads = width // SB_HEAD_DIM
    n_blocks = s // BLOCK
    group = math.gcd(n_blocks, ATTN_GROUP)
    blk = 2 * 4 * s * SB_HEAD_DIM * 2 + 2 * group * BLOCK * BLOCK * 4
    spec = lambda off: pl.BlockSpec((1, s, SB_HEAD_DIM), lambda bi, h: (bi, 0, off * heads + h))
    return pl.pallas_call(
        functools.partial(_sb_attn_kernel, n_blocks=n_blocks, group=group, scale=1.0 / math.sqrt(SB_HEAD_DIM)),
        grid=(b, heads),
        in_specs=[spec(0), spec(1), spec(2)],
        out_specs=pl.BlockSpec((1, s, SB_HEAD_DIM), lambda bi, h: (bi, 0, h)),
        out_shape=jax.ShapeDtypeStruct((b, s, width), BF16),
        scratch_shapes=[pltpu.VMEM((BLOCK, BLOCK), F32)] * (2 * group),
        compiler_params=pltpu.CompilerParams(
            dimension_semantics=("arbitrary", "arbitrary"), vmem_limit_bytes=_vmem_limit(blk)),
        name="sb_attention",
    )(qkv, qkv, qkv)


def _out_proj_ln_kernel(a1_ref, a2_ref, x_ref, w_ref, g_ref, b_ref, o_ref, *, alpha):
    k1 = a1_ref.shape[1]
    for rs in _even_slices(x_ref.shape[0]):
        y = (jnp.dot(a1_ref[rs, :], w_ref[:k1, :], preferred_element_type=F32)
             + jnp.dot(a2_ref[rs, :], w_ref[k1:, :], preferred_element_type=F32))
        o_ref[rs, :] = _layer_norm(alpha * x_ref[rs, :] + y, g_ref[...], b_ref[...])


def _out_proj_ln(a1, a2, x2, w_out, g, bvec, alpha, tm):
    m, d = x2.shape
    k1, k2 = a1.shape[1], a2.shape[1]
    blk = 2 * tm * (k1 + k2) * 2 + 2 * tm * d * 4 + 2 * (k1 + k2) * d * 2 + 2 * tm * d * 4 + tm * d * 4
    return pl.pallas_call(
        functools.partial(_out_proj_ln_kernel, alpha=alpha),
        grid=(m // tm,),
        in_specs=[pl.BlockSpec((tm, k1), lambda i: (i, 0)),
                  pl.BlockSpec((tm, k2), lambda i: (i, 0)),
                  pl.BlockSpec((tm, d), lambda i: (i, 0)),
                  _const_spec((k1 + k2, d)), _const_spec((1, d)), _const_spec((1, d))],
        out_specs=pl.BlockSpec((tm, d), lambda i: (i, 0)),
        out_shape=jax.ShapeDtypeStruct((m, d), F32),
        compiler_params=pltpu.CompilerParams(
            dimension_semantics=("arbitrary",), vmem_limit_bytes=_vmem_limit(blk)),
        name="out_proj_ln_even",
    )(a1, a2, x2, w_out, g.reshape(1, d), bvec.reshape(1, d))


def _ffn_kernel(h_ref, wg_ref, wu_ref, wd_ref, g_ref, b_ref, o_ref, hb_ref, *, alpha):
    j = pl.program_id(1)

    @pl.when(j == 0)
    def _():
        hb_ref[...] = h_ref[...].astype(BF16)
        o_ref[...] = jnp.zeros_like(o_ref)

    o_ref[...] += _swiglu_chunks(hb_ref[...], wg_ref, wu_ref, wd_ref, 0, 0, wd_ref.shape[0])

    @pl.when(j == pl.num_programs(1) - 1)
    def _():
        o_ref[...] = _layer_norm(alpha * h_ref[...] + o_ref[...], g_ref[...], b_ref[...])


def _ffn_ln(h2, w_gu, w_down, g, bvec, alpha, tm, tf):
    m, d = h2.shape
    f = w_down.shape[0]
    nf = f // tf
    blk = 2 * tm * d * 4 + 2 * 2 * d * tf * 2 + 2 * tf * d * 2 + 2 * tm * d * 4 + tm * d * 2 + 3 * tm * tf * 4
    return pl.pallas_call(
        functools.partial(_ffn_kernel, alpha=alpha),
        grid=(m // tm, nf),
        in_specs=[pl.BlockSpec((tm, d), lambda i, j: (i, 0)),
                  pl.BlockSpec((d, tf), lambda i, j: (0, j)),
                  pl.BlockSpec((d, tf), lambda i, j: (0, j + nf)),
                  pl.BlockSpec((tf, d), lambda i, j: (j, 0)),
                  _const_spec((1, d)), _const_spec((1, d))],
        out_specs=pl.BlockSpec((tm, d), lambda i, j: (i, 0)),
        out_shape=jax.ShapeDtypeStruct((m, d), F32),
        scratch_shapes=[pltpu.VMEM((tm, d), BF16)],
        compiler_params=pltpu.CompilerParams(
            dimension_semantics=("arbitrary", "arbitrary"), vmem_limit_bytes=_vmem_limit(blk)),
        name="ffn_ln_even",
    )(h2, w_gu, w_gu, w_down, g.reshape(1, d), bvec.reshape(1, d))


def _in_proj_odd_kernel(h_ref, w_ref, g_ref, b_ref, u_ref, v_ref):
    hb = h_ref[...].astype(BF16)
    width = u_ref.shape[1]
    v = _gelu_tanh(jnp.dot(hb, w_ref[:, width:], preferred_element_type=F32))
    v_ref[...] = _layer_norm(v, g_ref[...], b_ref[...]).astype(BF16)
    for cs in _even_slices(width):
        u_ref[:, cs] = _gelu_tanh(jnp.dot(hb, w_ref[:, cs], preferred_element_type=F32))


def _in_proj_odd(h2, w_in, g, bvec, tm):
    m, d = h2.shape
    n = w_in.shape[1]
    width = n // 2
    blk = 2 * tm * d * 4 + 2 * d * n * 2 + 2 * tm * width * 4 + 2 * tm * width * 2 + 2 * tm * width * 4
    return pl.pallas_call(
        _in_proj_odd_kernel,
        grid=(m // tm,),
        in_specs=[pl.BlockSpec((tm, d), lambda i: (i, 0)), _const_spec((d, n)),
                  _const_spec((1, width)), _const_spec((1, width))],
        out_specs=[pl.BlockSpec((tm, width), lambda i: (i, 0)),
                   pl.BlockSpec((tm, width), lambda i: (i, 0))],
        out_shape=[jax.ShapeDtypeStruct((m, width), F32), jax.ShapeDtypeStruct((m, width), BF16)],
        compiler_params=pltpu.CompilerParams(
            dimension_semantics=("arbitrary",), vmem_limit_bytes=_vmem_limit(blk)),
        name="in_proj_odd",
    )(h2, w_in, g.reshape(1, width), bvec.reshape(1, width))


def _sgu_out_ln_kernel(u_ref, v_ref, ws_ref, bs_ref, h_ref, w_ref, g_ref, b_ref, o_ref, gated_ref, *, alpha):
    tm, width = u_ref.shape
    gd = width // SGU_GROUPS
    row = lax.broadcasted_iota(jnp.int32, (SGU_CHUNK, SGU_CHUNK), 0)
    col = lax.broadcasted_iota(jnp.int32, (SGU_CHUNK, SGU_CHUNK), 1)
    lower = col <= row
    for gi in range(SGU_GROUPS):
        ws = jnp.where(lower, ws_ref[gi], 0.0).astype(BF16)
        bias = bs_ref[gi]
        cs = slice(gi * gd, (gi + 1) * gd)
        for n in range(tm // SGU_CHUNK):
            rs = slice(n * SGU_CHUNK, (n + 1) * SGU_CHUNK)
            mixed = jnp.dot(ws, v_ref[rs, cs], preferred_element_type=F32) + bias
            gated_ref[rs, cs] = (u_ref[rs, cs] * mixed).astype(BF16)
    for rs in _even_slices(tm):
        y = jnp.dot(gated_ref[rs, :], w_ref[...], preferred_element_type=F32)
        o_ref[rs, :] = _layer_norm(alpha * h_ref[rs, :] + y, g_ref[...], b_ref[...])


def _sgu_out_ln(u, v, w_s, b_s, h2, w_out, g, bvec, alpha, tm):
    m, d = h2.shape
    width = u.shape[1]
    blk = (2 * tm * width * 4 + 2 * tm * width * 2 + 2 * tm * d * 4 + 2 * width * d * 2 + 2 * tm * d * 4
           + tm * width * 2 + tm * d * 4 + 4 * SGU_GROUPS * SGU_CHUNK * V7X_LANES * 4)
    return pl.pallas_call(
        functools.partial(_sgu_out_ln_kernel, alpha=alpha),
        grid=(m // tm,),
        in_specs=[pl.BlockSpec((tm, width), lambda i: (i, 0)),
                  pl.BlockSpec((tm, width), lambda i: (i, 0)),
                  _const_spec((SGU_GROUPS, SGU_CHUNK, SGU_CHUNK)),
                  _const_spec((SGU_GROUPS, SGU_CHUNK, 1)),
                  pl.BlockSpec((tm, d), lambda i: (i, 0)),
                  _const_spec((width, d)), _const_spec((1, d)), _const_spec((1, d))],
        out_specs=pl.BlockSpec((tm, d), lambda i: (i, 0)),
        out_shape=jax.ShapeDtypeStruct((m, d), F32),
        scratch_shapes=[pltpu.VMEM((tm, width), BF16)],
        compiler_params=pltpu.CompilerParams(
            dimension_semantics=("arbitrary",), vmem_limit_bytes=_vmem_limit(blk)),
        name="sgu_out_ln_odd",
    )(u, v, w_s, b_s.reshape(SGU_GROUPS, SGU_CHUNK, 1), h2, w_out, g.reshape(1, d), bvec.reshape(1, d))


META_W1, META_W2, META_D1, META_D2 = range(4)
TAB_SIZE, TAB_OFF, TAB_RUN = 0, 8, 16
TAB_DST = TAB_RUN
GROUP_ALIGN = 16
MAX_EXPERTS = 8


def _grouped_rows(tm):
    return -(-(TOP_K * tm + MAX_EXPERTS * (GROUP_ALIGN - 1)) // V7X_LANES) * V7X_LANES


def _route_kernel(h_ref, wr_ref, meta_ref, post_ref, tab_ref, run_ref, *, n_experts):
    @pl.when(pl.program_id(0) == 0)
    def _():
        run_ref[...] = jnp.zeros_like(run_ref)

    tm = h_ref.shape[0]
    lanes = wr_ref.shape[1]
    parts = jnp.dot(jnp.concatenate(_split_bf16(h_ref[...]), axis=0),
                    jnp.concatenate(_split_bf16(wr_ref[...]), axis=1), preferred_element_type=F32)
    logits = (parts[:tm, :lanes] + parts[:tm, lanes:]) + (parts[tm:, :lanes] + parts[tm:, lanes:])
    lane = lax.broadcasted_iota(jnp.int32, logits.shape, 1)
    neg = jnp.float32(-jnp.inf)
    lg = jnp.where(lane < n_experts, logits, neg)
    m1 = jnp.max(lg, axis=1, keepdims=True)
    i1 = jnp.min(jnp.where(lg == m1, lane, lanes), axis=1, keepdims=True)
    lg2 = jnp.where(lane == i1, neg, lg)
    m2 = jnp.max(lg2, axis=1, keepdims=True)
    i2 = jnp.min(jnp.where(lg2 == m2, lane, lanes), axis=1, keepdims=True)
    e2 = jnp.exp(m2 - m1)
    denom = 1.0 + e2
    w1 = 1.0 / denom
    w2 = e2 / denom

    onehot = jnp.where((lane == i1) | (lane == i2), 1.0, 0.0)
    row = lax.broadcasted_iota(jnp.int32, (tm, tm), 0)
    col = lax.broadcasted_iota(jnp.int32, (tm, tm), 1)
    earlier = jnp.where(col < row, 1.0, 0.0).astype(BF16)
    within = jnp.dot(earlier, onehot.astype(BF16), preferred_element_type=F32)
    size = jnp.sum(onehot, axis=0, keepdims=True)
    size = jnp.ceil(size * (1.0 / GROUP_ALIGN)) * GROUP_ALIGN
    er = lax.broadcasted_iota(jnp.int32, (lanes, lanes), 0)
    ec = lax.broadcasted_iota(jnp.int32, (lanes, lanes), 1)
    before = jnp.where(er < ec, 1.0, 0.0).astype(BF16)
    size8 = jnp.broadcast_to(size, (8, lanes))
    off8 = jnp.dot(size8.astype(BF16), before, preferred_element_type=F32)
    where_in_tile = within + off8[0:1, :]
    d1 = jnp.sum(jnp.where(lane == i1, where_in_tile, 0.0), axis=1, keepdims=True)
    d2 = jnp.sum(jnp.where(lane == i2, where_in_tile, 0.0), axis=1, keepdims=True)

    meta = jnp.zeros_like(logits)
    for k, val in {META_W1: w1, META_W2: w2, META_D1: d1, META_D2: d2}.items():
        meta = jnp.where(lane == k, val, meta)
    meta_ref[...] = meta
    post_ref[...] = meta.T[:post_ref.shape[0], :].astype(jnp.int32)

    run8 = jnp.broadcast_to(run_ref[...], (8, lanes))
    tab = jnp.where(lane[:8] < TAB_OFF, size8, 0.0)
    tab = tab + pltpu.roll(jnp.where(lane[:8] < n_experts, off8, 0.0), TAB_OFF, 1)
    tab = tab + pltpu.roll(jnp.where(lane[:8] < n_experts, run8, 0.0), TAB_RUN, 1)
    tab_ref[0] = tab
    run_ref[...] += size


def _moe_route(h2, w_router, tm):
    m, d = h2.shape
    n_experts = w_router.shape[1]
    assert n_experts <= MAX_EXPERTS
    w_r = _pad_to(w_router, 1, V7X_LANES)
    lanes = w_r.shape[1]
    blk = 2 * tm * d * 4 + 2 * d * lanes * 4 + 4 * tm * lanes * 4 + tm * tm * 6 + 8 * tm * lanes * 4 + tm * d * 4
    return pl.pallas_call(
        functools.partial(_route_kernel, n_experts=n_experts),
        grid=(m // tm,),
        in_specs=[pl.BlockSpec((tm, d), lambda i: (i, 0)), _const_spec((d, lanes))],
        out_specs=[pl.BlockSpec((tm, lanes), lambda i: (i, 0)),
                   pl.BlockSpec((8, tm), lambda i: (0, i)),
                   pl.BlockSpec((1, 8, lanes), lambda i: (i, 0, 0))],
        out_shape=[jax.ShapeDtypeStruct((m, lanes), F32),
                   jax.ShapeDtypeStruct((8, m), jnp.int32),
                   jax.ShapeDtypeStruct((m // tm, 8, lanes), F32)],
        scratch_shapes=[pltpu.VMEM((1, lanes), F32)],
        compiler_params=pltpu.CompilerParams(
            dimension_semantics=("arbitrary",), vmem_limit_bytes=_vmem_limit(blk)),
        name="moe_route",
    )(h2, w_r)


def _group_copy(src_ref, src_row, dst_ref, dst_row, sem):
    return pltpu.make_async_copy(src_ref.at[pl.ds(src_row, GROUP_ALIGN), :],
                                 dst_ref.at[pl.ds(dst_row, GROUP_ALIGN), :], sem)


def _for_each_group_slab(tab_ref, i, n_experts, fn):
    for e in range(n_experts):
        size, off, dst = tab_ref[i, TAB_SIZE + e], tab_ref[i, TAB_OFF + e], tab_ref[i, TAB_DST + e]

        def slab(k, _):
            step = pl.multiple_of(k * GROUP_ALIGN, GROUP_ALIGN)
            fn(pl.multiple_of(off + step, GROUP_ALIGN), pl.multiple_of(dst + step, GROUP_ALIGN))
            return 0

        lax.fori_loop(0, size // GROUP_ALIGN, slab, 0)


def _split3_bf16(x):
    p0 = x.astype(BF16)
    r = x - p0.astype(F32)
    p1 = r.astype(BF16)
    p2 = (r - p1.astype(F32)).astype(BF16)
    return p0, p1, p2


def _dispatch_kernel(tab_ref, fill_ref, post_ref, h_ref, meta_ref, xs_ref, gs_ref,
                     grouped_ref, gate_ref, zero_ref, zero_gate_ref, sems, *, n_experts):
    i = pl.program_id(0)
    last = pl.num_programs(0) - 1
    buf = i % 2
    rows = grouped_ref.shape[1]
    tm = h_ref.shape[0]
    slot = lax.broadcasted_iota(jnp.int32, (rows, tm), 0)
    sel1 = jnp.where(slot == post_ref[META_D1:META_D1 + 1, :], 1.0, 0.0).astype(BF16)
    sel2 = jnp.where(slot == post_ref[META_D2:META_D2 + 1, :], 1.0, 0.0).astype(BF16)
    grouped_ref[buf] = jnp.dot(sel1 + sel2, h_ref[...].astype(BF16), preferred_element_type=F32).astype(BF16)
    meta = meta_ref[...]
    lane = lax.broadcasted_iota(jnp.int32, meta.shape, 1)
    pieces = []
    for w_lane in (META_W1, META_W2):
        wp = jnp.zeros(meta.shape, F32)
        for k, p in enumerate(_split3_bf16(meta[:, w_lane:w_lane + 1])):
            wp = jnp.where(lane == k, p.astype(F32), wp)
        pieces.append(wp.astype(BF16))
    gate_ref[buf] = (jnp.dot(sel1, pieces[0], preferred_element_type=F32)
                     + jnp.dot(sel2, pieces[1], preferred_element_type=F32))

    def copies(step, b, act):
        def both(src, dst):
            act(_group_copy(grouped_ref.at[b], src, xs_ref, dst, sems.at[b]))
            act(_group_copy(gate_ref.at[b], src, gs_ref, dst, sems.at[b]))
        _for_each_group_slab(tab_ref, step, n_experts, both)

    copies(i, buf, lambda c: c.start())

    @pl.when(i > 0)
    def _():
        copies(i - 1, 1 - buf, lambda c: c.wait())

    @pl.when(i == last)
    def _():
        copies(i, buf, lambda c: c.wait())
        zero_ref[...] = jnp.zeros_like(zero_ref)
        zero_gate_ref[...] = jnp.zeros_like(zero_gate_ref)
        for e in range(fill_ref.shape[1]):
            begin = fill_ref[0, e]
            n = (fill_ref[1, e] - begin) // GROUP_ALIGN

            def fill(k, act):
                dst = pl.multiple_of(begin + k * GROUP_ALIGN, GROUP_ALIGN)
                act(_group_copy(zero_ref, 0, xs_ref, dst, sems.at[0]))
                act(_group_copy(zero_gate_ref, 0, gs_ref, dst, sems.at[0]))
                return 0

            lax.fori_loop(0, n, lambda k, _: fill(k, lambda c: c.start()), 0)
            lax.fori_loop(0, n, lambda k, _: fill(k, lambda c: c.wait()), 0)


def _moe_dispatch(h2, post, meta, tab, fill, rows, n_experts, tm):
    m, d = h2.shape
    lanes = meta.shape[1]
    grouped = _grouped_rows(tm)
    blk = (2 * tm * d * 4 + 2 * 8 * tm * 4 + 2 * tm * lanes * 4 + 2 * grouped * d * 2 + 2 * grouped * lanes * 4
           + grouped * d * 4 + 3 * grouped * tm * 2 + tm * d * 2)
    return pl.pallas_call(
        functools.partial(_dispatch_kernel, n_experts=n_experts),
        grid=(m // tm,),
        in_specs=[pl.BlockSpec(memory_space=pltpu.SMEM),
                  pl.BlockSpec(memory_space=pltpu.SMEM),
                  pl.BlockSpec((8, tm), lambda i: (0, i)),
                  pl.BlockSpec((tm, d), lambda i: (i, 0)),
                  pl.BlockSpec((tm, lanes), lambda i: (i, 0))],
        out_specs=[pl.BlockSpec(memory_space=pl.ANY), pl.BlockSpec(memory_space=pl.ANY)],
        out_shape=[jax.ShapeDtypeStruct((rows, d), BF16), jax.ShapeDtypeStruct((rows, lanes), F32)],
        scratch_shapes=[pltpu.VMEM((2, grouped, d), BF16), pltpu.VMEM((2, grouped, lanes), F32),
                        pltpu.VMEM((GROUP_ALIGN, d), BF16), pltpu.VMEM((GROUP_ALIGN, lanes), F32),
                        pltpu.SemaphoreType.DMA((2,))],
        compiler_params=pltpu.CompilerParams(
            dimension_semantics=("arbitrary",), vmem_limit_bytes=_vmem_limit(blk)),
        name="moe_dispatch",
    )(tab, fill, post, h2, meta)


def _experts_kernel(te_ref, nt_ref, xs_ref, gs_ref, wgu_ref, wd_ref, ys_ref):
    occupied = pl.program_id(0) < nt_ref[0]

    @pl.when(occupied)
    def _():
        fe = wd_ref.shape[1]
        gate = jnp.sum(gs_ref[...], axis=1, keepdims=True)
        y = _swiglu_chunks(xs_ref[...], wgu_ref.at[0], wgu_ref.at[0], wd_ref.at[0], 0, fe, fe)
        ys_ref[...] = (gate * y).astype(ys_ref.dtype)

    @pl.when(jnp.logical_not(occupied))
    def _():
        ys_ref[...] = jnp.zeros_like(ys_ref)


def _moe_experts(xs, gs, w_gu, w_d, tile_expert, n_tiles, tile):
    rows, d = xs.shape
    lanes = gs.shape[1]
    _, _, fe2 = w_gu.shape
    fe = w_d.shape[1]
    blk = 2 * tile * d * 2 * 2 + 2 * d * fe2 * 2 + 2 * fe * d * 2 + tile * fe2 * 4 + tile * fe * 6 + tile * d * 4
    grid_spec = pltpu.PrefetchScalarGridSpec(
        num_scalar_prefetch=2,
        grid=(rows // tile,),
        in_specs=[pl.BlockSpec((tile, d), lambda i, te, nt: (i, 0)),
                  pl.BlockSpec((tile, lanes), lambda i, te, nt: (i, 0)),
                  pl.BlockSpec((1, d, fe2), lambda i, te, nt: (te[i], 0, 0)),
                  pl.BlockSpec((1, fe, d), lambda i, te, nt: (te[i], 0, 0))],
        out_specs=pl.BlockSpec((tile, d), lambda i, te, nt: (i, 0)),
    )
    return pl.pallas_call(
        _experts_kernel,
        grid_spec=grid_spec,
        out_shape=jax.ShapeDtypeStruct((rows, d), BF16),
        compiler_params=pltpu.CompilerParams(
            dimension_semantics=("arbitrary",), vmem_limit_bytes=_vmem_limit(blk)),
        name="moe_experts",
    )(tile_expert, n_tiles, xs, gs, w_gu, w_d)


def _combine_ln_kernel(tab_ref, h_ref, meta_ref, ys_ref, g_ref, b_ref, o_ref, grouped_ref, sems, *, alpha, n_experts):
    i = pl.program_id(0)
    buf = i % 2

    def copies(step, b, act):
        _for_each_group_slab(tab_ref, step, n_experts,
                             lambda row, src: act(_group_copy(ys_ref, src, grouped_ref.at[b], row, sems.at[b])))

    @pl.when(i == 0)
    def _():
        grouped_ref[...] = jnp.zeros_like(grouped_ref)
        copies(0, 0, lambda c: c.start())

    @pl.when(i + 1 < pl.num_programs(0))
    def _():
        copies(i + 1, 1 - buf, lambda c: c.start())

    copies(i, buf, lambda c: c.wait())

    tm = h_ref.shape[0]
    rows = grouped_ref.shape[1]
    meta = meta_ref[...]
    slot = lax.broadcasted_iota(jnp.int32, (tm, rows), 1)
    grouped = grouped_ref[buf]
    d1 = meta[:, META_D1:META_D1 + 1].astype(jnp.int32)
    d2 = meta[:, META_D2:META_D2 + 1].astype(jnp.int32)
    pick = jnp.where((slot == d1) | (slot == d2), 1.0, 0.0).astype(BF16)
    moe = jnp.dot(pick, grouped, preferred_element_type=F32)
    o_ref[...] = _layer_norm(alpha * h_ref[...] + moe, g_ref[...], b_ref[...])


def _moe_combine_ln(h2, tab, meta, ys, g, bvec, alpha, n_experts, tm):
    m, d = h2.shape
    lanes = meta.shape[1]
    grouped = _grouped_rows(tm)
    blk = 2 * tm * d * 4 * 2 + 2 * tm * lanes * 4 + 2 * grouped * d * 2 + 2 * tm * grouped * 2 + 3 * tm * d * 4
    return pl.pallas_call(
        functools.partial(_combine_ln_kernel, alpha=alpha, n_experts=n_experts),
        grid=(m // tm,),
        in_specs=[pl.BlockSpec(memory_space=pltpu.SMEM),
                  pl.BlockSpec((tm, d), lambda i: (i, 0)),
                  pl.BlockSpec((tm, lanes), lambda i: (i, 0)),
                  pl.BlockSpec(memory_space=pl.ANY),
                  _const_spec((1, d)), _const_spec((1, d))],
        out_specs=pl.BlockSpec((tm, d), lambda i: (i, 0)),
        out_shape=jax.ShapeDtypeStruct((m, d), F32),
        scratch_shapes=[pltpu.VMEM((2, grouped, d), BF16), pltpu.SemaphoreType.DMA((2,))],
        compiler_params=pltpu.CompilerParams(
            dimension_semantics=("arbitrary",), vmem_limit_bytes=_vmem_limit(blk)),
        name="moe_combine_ln",
    )(tab, h2, meta, ys, g.reshape(1, d), bvec.reshape(1, d))


def _moe_ln(h2, w_router, w_gu, w_d, g, bvec, alpha, tm, tile):
    m, d = h2.shape
    n_experts = w_router.shape[1]
    n_tok_tiles = m // tm
    meta, post, tab = _moe_route(h2, w_router, tm)

    tab = tab[:, 0, :].astype(jnp.int32)
    sizes = tab[:, TAB_SIZE:TAB_SIZE + n_experts]
    region = jnp.sum(sizes, axis=0)
    n_max = -(-(TOP_K * m + n_tok_tiles * n_experts * (GROUP_ALIGN - 1)) // tile) + n_experts
    rows = n_max * tile
    tiles = (region + (tile - 1)) // tile
    ends = jnp.cumsum(tiles)
    n_tiles = ends[-1]
    start = (ends - tiles) * tile
    tab = tab.at[:, TAB_DST:TAB_DST + n_experts].add(start[None, :])
    fill = jnp.stack([jnp.append(start + region, n_tiles * tile),
                      jnp.append(start + tiles * tile, rows)]).astype(jnp.int32)
    idx = jnp.minimum(jnp.arange(n_max, dtype=jnp.int32), n_tiles - 1)
    tile_expert = jnp.sum(idx[:, None] >= ends[None, :], axis=1).astype(jnp.int32)

    xs, gs = _moe_dispatch(h2, post, meta, tab, fill, rows, n_experts, tm)
    ys = _moe_experts(xs, gs, w_gu, w_d, tile_expert, n_tiles.reshape(1).astype(jnp.int32), tile)
    return _moe_combine_ln(h2, tab, meta, ys, g, bvec, alpha, n_experts, tm)


def kernel(x, even_w_in, even_pool_w, even_pool_scale, even_w_out, even_ln1_g, even_ln1_b, even_ffn_w_gu, even_ffn_w_down, even_ln2_g, even_ln2_b, odd_w_in, odd_sgu_ln_g, odd_sgu_ln_b, odd_sgu_w, odd_sgu_b, odd_w_out, odd_ln1_g, odd_ln1_b, odd_router, odd_moe_w_gu, odd_moe_w_down, odd_ln2_g, odd_ln2_b):
    b, s, d = x.shape
    depth = even_w_in.shape[0] + odd_w_in.shape[0]
    alpha = (2.0 * depth) ** 0.25
    pool_width = even_pool_scale.shape[1]
    sb_width = (even_w_in.shape[2] - pool_width) // 3
    m = b * s
    tm = min(ROW_TILE, m)

    h = x.reshape(m, d)
    for layer in range(depth):
        i = layer // 2
        if layer % 2 == 0:
            p, qkv = _in_proj_even(h, even_w_in[i].astype(BF16), pool_width, tm)
            pooled = _pool_mixer(p.reshape(b, s, pool_width), even_pool_w[i].astype(BF16),
                                 even_pool_scale[i], min(POOL_TILE, s))
            attn = _sb_attention(qkv.reshape(b, s, 3 * sb_width), sb_width)
            h = _out_proj_ln(pooled.reshape(m, pool_width), attn.reshape(m, sb_width), h,
                             even_w_out[i].astype(BF16), even_ln1_g[i], even_ln1_b[i], alpha, tm)
            h = _ffn_ln(h, even_ffn_w_gu[i].astype(BF16), even_ffn_w_down[i].astype(BF16),
                        even_ln2_g[i], even_ln2_b[i], alpha, tm, FFN_CHUNK)
        else:
            u, v = _in_proj_odd(h, odd_w_in[i].astype(BF16), odd_sgu_ln_g[i], odd_sgu_ln_b[i], tm)
            h = _sgu_out_ln(u, v, odd_sgu_w[i], odd_sgu_b[i], h, odd_w_out[i].astype(BF16),
                            odd_ln1_g[i], odd_ln1_b[i], alpha, tm)
            h = _moe_ln(h, odd_router[i], odd_moe_w_gu[i].astype(BF16), odd_moe_w_down[i].astype(BF16),
                        odd_ln2_g[i], odd_ln2_b[i], alpha, tm, min(EXPERT_TILE, m))
    return h.reshape(b, s, d)
```

```python
import functools
import math

import jax
import jax.numpy as jnp
from jax import lax
from jax.experimental import pallas as pl
from jax.experimental.pallas import tpu as pltpu

F32 = jnp.float32
BF16 = jnp.bfloat16

LN_EPS = 1e-5
POOL_WINDOWS = (2, 4, 8, 16)
SB_HEAD_DIM = 128
SGU_GROUPS = 8
SGU_CHUNK = 128
TOP_K = 2

V7X_LANES = 128
V7X_MXU_DIM = 256
V7X_VMEM_BYTES = 64 * 1024 * 1024
BLOCK = 128

ROW_TILE = 512
POOL_TILE = 1024
FFN_CHUNK = 512
EXPERT_TILE = 512
ATTN_GROUP = 64

F32_EXP_UNDERFLOW = -104.0
NO_WEIGHT = -1e30


def _vmem_limit(block_bytes):
    return int(min(block_bytes * 1.3 + (6 << 20), V7X_VMEM_BYTES - (3 << 20)))


def _layer_norm(r, g, b):
    mu = jnp.mean(r, axis=-1, keepdims=True)
    xc = r - mu
    var = jnp.mean(xc * xc, axis=-1, keepdims=True)
    return xc * lax.rsqrt(var + LN_EPS) * g + b


def _silu(x):
    return x * jax.nn.sigmoid(x)


def _gelu_tanh(x):
    c = math.sqrt(2.0 / math.pi)
    return 0.5 * x * (1.0 + jnp.tanh(c * (x + 0.044715 * (x * x * x))))


def _swiglu_chunks(xb, wg_ref, wu_ref, wd_ref, gate_col, up_col, width, chunk=V7X_MXU_DIM):
    y = None
    for lo in range(0, width, chunk):
        n = min(chunk, width - lo)
        gate = jnp.dot(xb, wg_ref[:, gate_col + lo:gate_col + lo + n], preferred_element_type=F32)
        up = jnp.dot(xb, wu_ref[:, up_col + lo:up_col + lo + n], preferred_element_type=F32)
        act = (_silu(gate) * up).astype(BF16)
        part = jnp.dot(act, wd_ref[lo:lo + n, :], preferred_element_type=F32)
        y = part if y is None else y + part
    return y


def _split_bf16(x):
    hi = x.astype(BF16)
    lo = (x - hi.astype(F32)).astype(BF16)
    return hi, lo


def _even_slices(rows, parts=2):
    step = rows // parts
    return [slice(k * step, (k + 1) * step) for k in range(parts)]


def _const_spec(shape):
    nd = len(shape)
    return pl.BlockSpec(shape, lambda *_: (0,) * nd)


def _pad_to(x, axis, mult):
    pad = (-x.shape[axis]) % mult
    if pad == 0:
        return x
    widths = [(0, 0)] * x.ndim
    widths[axis] = (0, pad)
    return jnp.pad(x, widths)


def _in_proj_even_kernel(x_ref, w_ref, p_ref, qkv_ref, *, pool_width):
    xb = x_ref[...].astype(BF16)
    p_ref[...] = jnp.dot(xb, w_ref[:, :pool_width], preferred_element_type=F32)
    n_rest = (w_ref.shape[1] - pool_width) // pool_width
    for c in range(n_rest):
        lo = pool_width * (c + 1)
        qkv_ref[:, c * pool_width:(c + 1) * pool_width] = jnp.dot(
            xb, w_ref[:, lo:lo + pool_width], preferred_element_type=F32).astype(BF16)


def _in_proj_even(x2, w_in, pool_width, tm):
    m, d = x2.shape
    n = w_in.shape[1]
    blk = 2 * tm * d * 4 + 2 * d * n * 2 + 2 * tm * pool_width * 4 + 2 * tm * (n - pool_width) * 2 + tm * n * 4
    return pl.pallas_call(
        functools.partial(_in_proj_even_kernel, pool_width=pool_width),
        grid=(m // tm,),
        in_specs=[pl.BlockSpec((tm, d), lambda i: (i, 0)), _const_spec((d, n))],
        out_specs=[pl.BlockSpec((tm, pool_width), lambda i: (i, 0)),
                   pl.BlockSpec((tm, n - pool_width), lambda i: (i, 0))],
        out_shape=[jax.ShapeDtypeStruct((m, pool_width), F32),
                   jax.ShapeDtypeStruct((m, n - pool_width), BF16)],
        compiler_params=pltpu.CompilerParams(
            dimension_semantics=("arbitrary",), vmem_limit_bytes=_vmem_limit(blk)),
        name="in_proj_even",
    )(x2, w_in)


def _pool_kernel(pc_ref, pp_ref, w_ref, sc_ref, o_ref, *, ts, group_dim):
    i = pl.program_id(1)
    row = lax.broadcasted_iota(jnp.int32, (ts, 1), 0)
    t = i * ts + row
    for g, w in enumerate(POOL_WINDOWS):
        cs = slice(g * group_dim, (g + 1) * group_dim)
        cur = pc_ref[0, :, cs]
        prev = jnp.where(i > 0, pp_ref[0, :, cs], 0.0)
        acc = jnp.concatenate([prev, cur], axis=0)
        k = 1
        while k < w:
            acc = acc + pltpu.roll(acc, k, 0)
            k *= 2
        count = jnp.minimum(t + 1, w).astype(F32)
        d = acc[BLOCK:] / count - cur
        y = jnp.dot(d.astype(BF16), w_ref[g], preferred_element_type=F32) * sc_ref[:, cs]
        o_ref[0, :, cs] = y.astype(o_ref.dtype)


def _pool_mixer(p, pool_w, pool_scale, ts):
    b, s, c = p.shape
    groups = len(POOL_WINDOWS)
    gd = c // groups
    per = ts // BLOCK
    blk = 2 * ts * c * 4 + 2 * BLOCK * c * 4 + 2 * groups * gd * gd * 2 + 2 * ts * c * 2 + 6 * (ts + BLOCK) * gd * 4
    return pl.pallas_call(
        functools.partial(_pool_kernel, ts=ts, group_dim=gd),
        grid=(b, s // ts),
        in_specs=[pl.BlockSpec((1, ts, c), lambda bi, i: (bi, i, 0)),
                  pl.BlockSpec((1, BLOCK, c), lambda bi, i: (bi, jnp.maximum(i * per - 1, 0), 0)),
                  _const_spec((groups, gd, gd)),
                  _const_spec((1, c))],
        out_specs=pl.BlockSpec((1, ts, c), lambda bi, i: (bi, i, 0)),
        out_shape=jax.ShapeDtypeStruct((b, s, c), BF16),
        compiler_params=pltpu.CompilerParams(
            dimension_semantics=("arbitrary", "arbitrary"), vmem_limit_bytes=_vmem_limit(blk)),
        name="pool_mixer",
    )(p, p, pool_w, pool_scale.reshape(1, c))


def _sb_attn_kernel(q_ref, k_ref, v_ref, o_ref, *scratch, n_blocks, group, scale):
    acc_refs, carry_refs = scratch[:group], scratch[group:]
    row = lax.broadcasted_iota(jnp.int32, (BLOCK, BLOCK), 0)
    col = lax.broadcasted_iota(jnp.int32, (BLOCK, BLOCK), 1)
    causal = col < row
    jj = lax.broadcasted_iota(jnp.int32, (BLOCK, 2 * BLOCK), 0)
    ss = lax.broadcasted_iota(jnp.int32, (BLOCK, 2 * BLOCK), 1)
    tmat = jnp.where((jj > ss) | (ss >= BLOCK), 1.0, 0.0).astype(BF16)
    tmat2 = jnp.concatenate([tmat, tmat], axis=0)

    def tiles(qs, ks, vs, carries, mask):
        n = len(qs)
        log_betas, parts = [], []
        for q, kj in zip(qs, ks):
            z = lax.dot_general(q, kj, (((1,), (1,)), ((), ())), preferred_element_type=F32) * scale
            sp = jnp.maximum(z, 0.0) + jnp.log(1.0 + jnp.exp(-jnp.abs(z)))
            log_betas.append(z - sp)
            if mask is not None:
                sp = jnp.where(mask, sp, 0.0)
            parts.append(jnp.concatenate(_split_bf16(sp), axis=1))
        sums = jnp.dot(jnp.concatenate(parts, axis=0), tmat2, preferred_element_type=F32)
        pvs, new_carries = [], []
        for r in range(n):
            s = sums[r * BLOCK:(r + 1) * BLOCK]
            a = jnp.exp(log_betas[r] - s[:, :BLOCK] + carries[r])
            if mask is not None:
                a = jnp.where(mask, a, 0.0)
            pvs.append(jnp.dot(a.astype(BF16), vs[r], preferred_element_type=F32))
            new_carries.append(carries[r] - s[:, BLOCK:])
        return pvs, new_carries

    def top_of(carries):
        m = carries[0]
        for c in carries[1:]:
            m = jnp.maximum(m, c)
        return jnp.max(m)

    def q_group(gi, _):
        base = gi * group
        offs = [pl.multiple_of((base + r) * BLOCK, BLOCK) for r in range(group)]
        load = lambda ref, off: ref[0, pl.ds(off, BLOCK), :]
        pvs, carries = tiles([load(q_ref, o) for o in offs], [load(k_ref, o) for o in offs],
                             [load(v_ref, o) for o in offs], [jnp.zeros((BLOCK, BLOCK), F32)] * group, causal)
        for r in range(group):
            acc_refs[r][...] = pvs[r]
            carry_refs[r][...] = carries[r]

        def cond(c):
            k, top = c
            return jnp.logical_and(k < base + group, top > F32_EXP_UNDERFLOW)

        def body(c):
            k, _ = c
            koffs, carries = [], []
            for r in range(group):
                j = base + r - k
                carries.append(jnp.where(j >= 0, carry_refs[r][...], NO_WEIGHT))
                koffs.append(pl.multiple_of(jnp.maximum(j, 0) * BLOCK, BLOCK))
            pvs, carries = tiles([load(q_ref, o) for o in offs], [load(k_ref, o) for o in koffs],
                                 [load(v_ref, o) for o in koffs], carries, None)
            for r in range(group):
                acc_refs[r][...] += pvs[r]
                carry_refs[r][...] = carries[r]
            return k + 1, top_of(carries)

        lax.while_loop(cond, body, (jnp.int32(1), top_of(carries)))
        for r in range(group):
            o_ref[0, pl.ds(offs[r], BLOCK), :] = acc_refs[r][...].astype(o_ref.dtype)
        return 0

    lax.fori_loop(0, n_blocks // group, q_group, 0)


def _sb_attention(qkv, width):
    b, s, _ = qkv.shape
    heads = width // SB_HEAD_DIM
    n_blocks = s // BLOCK
    group = math.gcd(n_blocks, ATTN_GROUP)
    blk = 2 * 4 * s * SB_HEAD_DIM * 2 + 2 * group * BLOCK * BLOCK * 4
    spec = lambda off: pl.BlockSpec((1, s, SB_HEAD_DIM), lambda bi, h: (bi, 0, off * heads + h))
    return pl.pallas_call(
        functools.partial(_sb_attn_kernel, n_blocks=n_blocks, group=group, scale=1.0 / math.sqrt(SB_HEAD_DIM)),
        grid=(b, heads),
        in_specs=[spec(0), spec(1), spec(2)],
        out_specs=pl.BlockSpec((1, s, SB_HEAD_DIM), lambda bi, h: (bi, 0, h)),
        out_shape=jax.ShapeDtypeStruct((b, s, width), BF16),
        scratch_shapes=[pltpu.VMEM((BLOCK, BLOCK), F32)] * (2 * group),
        compiler_params=pltpu.CompilerParams(
            dimension_semantics=("arbitrary", "arbitrary"), vmem_limit_bytes=_vmem_limit(blk)),
        name="sb_attention",
    )(qkv, qkv, qkv)


def _out_proj_ln_kernel(a1_ref, a2_ref, x_ref, w_ref, g_ref, b_ref, o_ref, *, alpha):
    k1 = a1_ref.shape[1]
    for rs in _even_slices(x_ref.shape[0]):
        y = (jnp.dot(a1_ref[rs, :], w_ref[:k1, :], preferred_element_type=F32)
             + jnp.dot(a2_ref[rs, :], w_ref[k1:, :], preferred_element_type=F32))
        o_ref[rs, :] = _layer_norm(alpha * x_ref[rs, :] + y, g_ref[...], b_ref[...])


def _out_proj_ln(a1, a2, x2, w_out, g, bvec, alpha, tm):
    m, d = x2.shape
    k1, k2 = a1.shape[1], a2.shape[1]
    blk = 2 * tm * (k1 + k2) * 2 + 2 * tm * d * 4 + 2 * (k1 + k2) * d * 2 + 2 * tm * d * 4 + tm * d * 4
    return pl.pallas_call(
        functools.partial(_out_proj_ln_kernel, alpha=alpha),
        grid=(m // tm,),
        in_specs=[pl.BlockSpec((tm, k1), lambda i: (i, 0)),
                  pl.BlockSpec((tm, k2), lambda i: (i, 0)),
                  pl.BlockSpec((tm, d), lambda i: (i, 0)),
                  _const_spec((k1 + k2, d)), _const_spec((1, d)), _const_spec((1, d))],
        out_specs=pl.BlockSpec((tm, d), lambda i: (i, 0)),
        out_shape=jax.ShapeDtypeStruct((m, d), F32),
        compiler_params=pltpu.CompilerParams(
            dimension_semantics=("arbitrary",), vmem_limit_bytes=_vmem_limit(blk)),
        name="out_proj_ln_even",
    )(a1, a2, x2, w_out, g.reshape(1, d), bvec.reshape(1, d))


def _ffn_kernel(h_ref, wg_ref, wu_ref, wd_ref, g_ref, b_ref, o_ref, hb_ref, *, alpha):
    j = pl.program_id(1)

    @pl.when(j == 0)
    def _():
        hb_ref[...] = h_ref[...].astype(BF16)
        o_ref[...] = jnp.zeros_like(o_ref)

    o_ref[...] += _swiglu_chunks(hb_ref[...], wg_ref, wu_ref, wd_ref, 0, 0, wd_ref.shape[0])

    @pl.when(j == pl.num_programs(1) - 1)
    def _():
        o_ref[...] = _layer_norm(alpha * h_ref[...] + o_ref[...], g_ref[...], b_ref[...])


def _ffn_ln(h2, w_gu, w_down, g, bvec, alpha, tm, tf):
    m, d = h2.shape
    f = w_down.shape[0]
    nf = f // tf
    blk = 2 * tm * d * 4 + 2 * 2 * d * tf * 2 + 2 * tf * d * 2 + 2 * tm * d * 4 + tm * d * 2 + 3 * tm * tf * 4
    return pl.pallas_call(
        functools.partial(_ffn_kernel, alpha=alpha),
        grid=(m // tm, nf),
        in_specs=[pl.BlockSpec((tm, d), lambda i, j: (i, 0)),
                  pl.BlockSpec((d, tf), lambda i, j: (0, j)),
                  pl.BlockSpec((d, tf), lambda i, j: (0, j + nf)),
                  pl.BlockSpec((tf, d), lambda i, j: (j, 0)),
                  _const_spec((1, d)), _const_spec((1, d))],
        out_specs=pl.BlockSpec((tm, d), lambda i, j: (i, 0)),
        out_shape=jax.ShapeDtypeStruct((m, d), F32),
        scratch_shapes=[pltpu.VMEM((tm, d), BF16)],
        compiler_params=pltpu.CompilerParams(
            dimension_semantics=("arbitrary", "arbitrary"), vmem_limit_bytes=_vmem_limit(blk)),
        name="ffn_ln_even",
    )(h2, w_gu, w_gu, w_down, g.reshape(1, d), bvec.reshape(1, d))


def _in_proj_odd_kernel(h_ref, w_ref, g_ref, b_ref, u_ref, v_ref):
    width = u_ref.shape[1]
    rows = _even_slices(h_ref.shape[0])
    hbs = [h_ref[rs, :].astype(BF16) for rs in rows]
    for rs, hb in zip(rows, hbs):
        v = _gelu_tanh(jnp.dot(hb, w_ref[:, width:], preferred_element_type=F32))
        v_ref[rs, :] = _layer_norm(v, g_ref[...], b_ref[...]).astype(BF16)
    for rs, hb in zip(rows, hbs):
        u_ref[rs, :] = _gelu_tanh(jnp.dot(hb, w_ref[:, :width], preferred_element_type=F32))


def _in_proj_odd(h2, w_in, g, bvec, tm):
    m, d = h2.shape
    n = w_in.shape[1]
    width = n // 2
    blk = 2 * tm * d * 4 + 2 * d * n * 2 + 2 * tm * width * 4 + 2 * tm * width * 2 + 2 * tm * width * 4
    return pl.pallas_call(
        _in_proj_odd_kernel,
        grid=(m // tm,),
        in_specs=[pl.BlockSpec((tm, d), lambda i: (i, 0)), _const_spec((d, n)),
                  _const_spec((1, width)), _const_spec((1, width))],
        out_specs=[pl.BlockSpec((tm, width), lambda i: (i, 0)),
                   pl.BlockSpec((tm, width), lambda i: (i, 0))],
        out_shape=[jax.ShapeDtypeStruct((m, width), F32), jax.ShapeDtypeStruct((m, width), BF16)],
        compiler_params=pltpu.CompilerParams(
            dimension_semantics=("arbitrary",), vmem_limit_bytes=_vmem_limit(blk)),
        name="in_proj_odd",
    )(h2, w_in, g.reshape(1, width), bvec.reshape(1, width))


def _sgu_out_ln_kernel(u_ref, v_ref, ws_ref, bs_ref, h_ref, w_ref, g_ref, b_ref, o_ref, gated_ref, *, alpha):
    tm, width = u_ref.shape
    gd = width // SGU_GROUPS
    row = lax.broadcasted_iota(jnp.int32, (SGU_CHUNK, SGU_CHUNK), 0)
    col = lax.broadcasted_iota(jnp.int32, (SGU_CHUNK, SGU_CHUNK), 1)
    lower = col <= row
    for gi in range(SGU_GROUPS):
        ws = jnp.where(lower, ws_ref[gi], 0.0).astype(BF16)
        bias = bs_ref[gi]
        cs = slice(gi * gd, (gi + 1) * gd)
        for n in range(tm // SGU_CHUNK):
            rs = slice(n * SGU_CHUNK, (n + 1) * SGU_CHUNK)
            mixed = jnp.dot(ws, v_ref[rs, cs], preferred_element_type=F32) + bias
            gated_ref[rs, cs] = (u_ref[rs, cs] * mixed).astype(BF16)
    for rs in _even_slices(tm):
        y = jnp.dot(gated_ref[rs, :], w_ref[...], preferred_element_type=F32)
        o_ref[rs, :] = _layer_norm(alpha * h_ref[rs, :] + y, g_ref[...], b_ref[...])


def _sgu_out_ln(u, v, w_s, b_s, h2, w_out, g, bvec, alpha, tm):
    m, d = h2.shape
    width = u.shape[1]
    blk = (2 * tm * width * 4 + 2 * tm * width * 2 + 2 * tm * d * 4 + 2 * width * d * 2 + 2 * tm * d * 4
           + tm * width * 2 + tm * d * 4 + 4 * SGU_GROUPS * SGU_CHUNK * V7X_LANES * 4)
    return pl.pallas_call(
        functools.partial(_sgu_out_ln_kernel, alpha=alpha),
        grid=(m // tm,),
        in_specs=[pl.BlockSpec((tm, width), lambda i: (i, 0)),
                  pl.BlockSpec((tm, width), lambda i: (i, 0)),
                  _const_spec((SGU_GROUPS, SGU_CHUNK, SGU_CHUNK)),
                  _const_spec((SGU_GROUPS, SGU_CHUNK, 1)),
                  pl.BlockSpec((tm, d), lambda i: (i, 0)),
                  _const_spec((width, d)), _const_spec((1, d)), _const_spec((1, d))],
        out_specs=pl.BlockSpec((tm, d), lambda i: (i, 0)),
        out_shape=jax.ShapeDtypeStruct((m, d), F32),
        scratch_shapes=[pltpu.VMEM((tm, width), BF16)],
        compiler_params=pltpu.CompilerParams(
            dimension_semantics=("arbitrary",), vmem_limit_bytes=_vmem_limit(blk)),
        name="sgu_out_ln_odd",
    )(u, v, w_s, b_s.reshape(SGU_GROUPS, SGU_CHUNK, 1), h2, w_out, g.reshape(1, d), bvec.reshape(1, d))


META_W1, META_W2, META_D1, META_D2 = range(4)
TAB_SIZE, TAB_OFF, TAB_RUN = 0, 8, 16
TAB_DST = TAB_RUN
GROUP_ALIGN = 16
MAX_EXPERTS = 8


def _grouped_rows(tm):
    return -(-(TOP_K * tm + MAX_EXPERTS * (GROUP_ALIGN - 1)) // V7X_LANES) * V7X_LANES


def _route_kernel(h_ref, wr_ref, meta_ref, post_ref, tab_ref, run_ref, *, n_experts):
    @pl.when(pl.program_id(0) == 0)
    def _():
        run_ref[...] = jnp.zeros_like(run_ref)

    tm = h_ref.shape[0]
    lanes = wr_ref.shape[1]
    parts = jnp.dot(jnp.concatenate(_split_bf16(h_ref[...]), axis=0),
                    jnp.concatenate(_split_bf16(wr_ref[...]), axis=1), preferred_element_type=F32)
    logits = (parts[:tm, :lanes] + parts[:tm, lanes:]) + (parts[tm:, :lanes] + parts[tm:, lanes:])
    lane = lax.broadcasted_iota(jnp.int32, logits.shape, 1)
    neg = jnp.float32(-jnp.inf)
    lg = jnp.where(lane < n_experts, logits, neg)
    m1 = jnp.max(lg, axis=1, keepdims=True)
    i1 = jnp.min(jnp.where(lg == m1, lane, lanes), axis=1, keepdims=True)
    lg2 = jnp.where(lane == i1, neg, lg)
    m2 = jnp.max(lg2, axis=1, keepdims=True)
    i2 = jnp.min(jnp.where(lg2 == m2, lane, lanes), axis=1, keepdims=True)
    e2 = jnp.exp(m2 - m1)
    denom = 1.0 + e2
    w1 = 1.0 / denom
    w2 = e2 / denom

    onehot = jnp.where((lane == i1) | (lane == i2), 1.0, 0.0)
    row = lax.broadcasted_iota(jnp.int32, (tm, tm), 0)
    col = lax.broadcasted_iota(jnp.int32, (tm, tm), 1)
    earlier = jnp.where(col < row, 1.0, 0.0).astype(BF16)
    within = jnp.dot(earlier, onehot.astype(BF16), preferred_element_type=F32)
    size = jnp.sum(onehot, axis=0, keepdims=True)
    size = jnp.ceil(size * (1.0 / GROUP_ALIGN)) * GROUP_ALIGN
    er = lax.broadcasted_iota(jnp.int32, (lanes, lanes), 0)
    ec = lax.broadcasted_iota(jnp.int32, (lanes, lanes), 1)
    before = jnp.where(er < ec, 1.0, 0.0).astype(BF16)
    size8 = jnp.broadcast_to(size, (8, lanes))
    off8 = jnp.dot(size8.astype(BF16), before, preferred_element_type=F32)
    where_in_tile = within + off8[0:1, :]
    d1 = jnp.sum(jnp.where(lane == i1, where_in_tile, 0.0), axis=1, keepdims=True)
    d2 = jnp.sum(jnp.where(lane == i2, where_in_tile, 0.0), axis=1, keepdims=True)

    meta = jnp.zeros_like(logits)
    for k, val in {META_W1: w1, META_W2: w2, META_D1: d1, META_D2: d2}.items():
        meta = jnp.where(lane == k, val, meta)
    meta_ref[...] = meta
    post_ref[...] = meta.T[:post_ref.shape[0], :].astype(jnp.int32)

    run8 = jnp.broadcast_to(run_ref[...], (8, lanes))
    tab = jnp.where(lane[:8] < TAB_OFF, size8, 0.0)
    tab = tab + pltpu.roll(jnp.where(lane[:8] < n_experts, off8, 0.0), TAB_OFF, 1)
    tab = tab + pltpu.roll(jnp.where(lane[:8] < n_experts, run8, 0.0), TAB_RUN, 1)
    tab_ref[0] = tab
    run_ref[...] += size


def _moe_route(h2, w_router, tm):
    m, d = h2.shape
    n_experts = w_router.shape[1]
    assert n_experts <= MAX_EXPERTS
    w_r = _pad_to(w_router, 1, V7X_LANES)
    lanes = w_r.shape[1]
    blk = 2 * tm * d * 4 + 2 * d * lanes * 4 + 4 * tm * lanes * 4 + tm * tm * 6 + 8 * tm * lanes * 4 + tm * d * 4
    return pl.pallas_call(
        functools.partial(_route_kernel, n_experts=n_experts),
        grid=(m // tm,),
        in_specs=[pl.BlockSpec((tm, d), lambda i: (i, 0)), _const_spec((d, lanes))],
        out_specs=[pl.BlockSpec((tm, lanes), lambda i: (i, 0)),
                   pl.BlockSpec((8, tm), lambda i: (0, i)),
                   pl.BlockSpec((1, 8, lanes), lambda i: (i, 0, 0))],
        out_shape=[jax.ShapeDtypeStruct((m, lanes), F32),
                   jax.ShapeDtypeStruct((8, m), jnp.int32),
                   jax.ShapeDtypeStruct((m // tm, 8, lanes), F32)],
        scratch_shapes=[pltpu.VMEM((1, lanes), F32)],
        compiler_params=pltpu.CompilerParams(
            dimension_semantics=("arbitrary",), vmem_limit_bytes=_vmem_limit(blk)),
        name="moe_route",
    )(h2, w_r)


def _group_copy(src_ref, src_row, dst_ref, dst_row, sem):
    return pltpu.make_async_copy(src_ref.at[pl.ds(src_row, GROUP_ALIGN), :],
                                 dst_ref.at[pl.ds(dst_row, GROUP_ALIGN), :], sem)


def _for_each_group_slab(tab_ref, i, n_experts, fn):
    for e in range(n_experts):
        size, off, dst = tab_ref[i, TAB_SIZE + e], tab_ref[i, TAB_OFF + e], tab_ref[i, TAB_DST + e]

        def slab(k, _):
            step = pl.multiple_of(k * GROUP_ALIGN, GROUP_ALIGN)
            fn(pl.multiple_of(off + step, GROUP_ALIGN), pl.multiple_of(dst + step, GROUP_ALIGN))
            return 0

        lax.fori_loop(0, size // GROUP_ALIGN, slab, 0)


def _split3_bf16(x):
    p0 = x.astype(BF16)
    r = x - p0.astype(F32)
    p1 = r.astype(BF16)
    p2 = (r - p1.astype(F32)).astype(BF16)
    return p0, p1, p2


def _dispatch_kernel(tab_ref, fill_ref, post_ref, h_ref, meta_ref, xs_ref, gs_ref,
                     grouped_ref, gate_ref, zero_ref, zero_gate_ref, sems, *, n_experts):
    i = pl.program_id(0)
    last = pl.num_programs(0) - 1
    buf = i % 2
    rows = grouped_ref.shape[1]
    tm = h_ref.shape[0]
    slot = lax.broadcasted_iota(jnp.int32, (rows, tm), 0)
    sel1 = jnp.where(slot == post_ref[META_D1:META_D1 + 1, :], 1.0, 0.0).astype(BF16)
    sel2 = jnp.where(slot == post_ref[META_D2:META_D2 + 1, :], 1.0, 0.0).astype(BF16)
    grouped_ref[buf] = jnp.dot(sel1 + sel2, h_ref[...].astype(BF16), preferred_element_type=F32).astype(BF16)
    meta = meta_ref[...]
    lane = lax.broadcasted_iota(jnp.int32, meta.shape, 1)
    pieces = []
    for w_lane in (META_W1, META_W2):
        wp = jnp.zeros(meta.shape, F32)
        for k, p in enumerate(_split3_bf16(meta[:, w_lane:w_lane + 1])):
            wp = jnp.where(lane == k, p.astype(F32), wp)
        pieces.append(wp.astype(BF16))
    gate_ref[buf] = (jnp.dot(sel1, pieces[0], preferred_element_type=F32)
                     + jnp.dot(sel2, pieces[1], preferred_element_type=F32))

    def copies(step, b, act):
        def both(src, dst):
            act(_group_copy(grouped_ref.at[b], src, xs_ref, dst, sems.at[b]))
            act(_group_copy(gate_ref.at[b], src, gs_ref, dst, sems.at[b]))
        _for_each_group_slab(tab_ref, step, n_experts, both)

    copies(i, buf, lambda c: c.start())

    @pl.when(i > 0)
    def _():
        copies(i - 1, 1 - buf, lambda c: c.wait())

    @pl.when(i == last)
    def _():
        copies(i, buf, lambda c: c.wait())
        zero_ref[...] = jnp.zeros_like(zero_ref)
        zero_gate_ref[...] = jnp.zeros_like(zero_gate_ref)
        for e in range(fill_ref.shape[1]):
            begin = fill_ref[0, e]
            n = (fill_ref[1, e] - begin) // GROUP_ALIGN

            def fill(k, act):
                dst = pl.multiple_of(begin + k * GROUP_ALIGN, GROUP_ALIGN)
                act(_group_copy(zero_ref, 0, xs_ref, dst, sems.at[0]))
                act(_group_copy(zero_gate_ref, 0, gs_ref, dst, sems.at[0]))
                return 0

            lax.fori_loop(0, n, lambda k, _: fill(k, lambda c: c.start()), 0)
            lax.fori_loop(0, n, lambda k, _: fill(k, lambda c: c.wait()), 0)


def _moe_dispatch(h2, post, meta, tab, fill, rows, n_experts, tm):
    m, d = h2.shape
    lanes = meta.shape[1]
    grouped = _grouped_rows(tm)
    blk = (2 * tm * d * 4 + 2 * 8 * tm * 4 + 2 * tm * lanes * 4 + 2 * grouped * d * 2 + 2 * grouped * lanes * 4
           + grouped * d * 4 + 3 * grouped * tm * 2 + tm * d * 2)
    return pl.pallas_call(
        functools.partial(_dispatch_kernel, n_experts=n_experts),
        grid=(m // tm,),
        in_specs=[pl.BlockSpec(memory_space=pltpu.SMEM),
                  pl.BlockSpec(memory_space=pltpu.SMEM),
                  pl.BlockSpec((8, tm), lambda i: (0, i)),
                  pl.BlockSpec((tm, d), lambda i: (i, 0)),
                  pl.BlockSpec((tm, lanes), lambda i: (i, 0))],
        out_specs=[pl.BlockSpec(memory_space=pl.ANY), pl.BlockSpec(memory_space=pl.ANY)],
        out_shape=[jax.ShapeDtypeStruct((rows, d), BF16), jax.ShapeDtypeStruct((rows, lanes), F32)],
        scratch_shapes=[pltpu.VMEM((2, grouped, d), BF16), pltpu.VMEM((2, grouped, lanes), F32),
                        pltpu.VMEM((GROUP_ALIGN, d), BF16), pltpu.VMEM((GROUP_ALIGN, lanes), F32),
                        pltpu.SemaphoreType.DMA((2,))],
        compiler_params=pltpu.CompilerParams(
            dimension_semantics=("arbitrary",), vmem_limit_bytes=_vmem_limit(blk)),
        name="moe_dispatch",
    )(tab, fill, post, h2, meta)


def _experts_kernel(te_ref, nt_ref, xs_ref, gs_ref, wgu_ref, wd_ref, ys_ref):
    occupied = pl.program_id(0) < nt_ref[0]

    @pl.when(occupied)
    def _():
        fe = wd_ref.shape[1]
        gate = jnp.sum(gs_ref[...], axis=1, keepdims=True)
        y = _swiglu_chunks(xs_ref[...], wgu_ref.at[0], wgu_ref.at[0], wd_ref.at[0], 0, fe, fe)
        ys_ref[...] = (gate * y).astype(ys_ref.dtype)

    @pl.when(jnp.logical_not(occupied))
    def _():
        ys_ref[...] = jnp.zeros_like(ys_ref)


def _moe_experts(xs, gs, w_gu, w_d, tile_expert, n_tiles, tile):
    rows, d = xs.shape
    lanes = gs.shape[1]
    _, _, fe2 = w_gu.shape
    fe = w_d.shape[1]
    blk = 2 * tile * d * 2 * 2 + 2 * d * fe2 * 2 + 2 * fe * d * 2 + tile * fe2 * 4 + tile * fe * 6 + tile * d * 4
    grid_spec = pltpu.PrefetchScalarGridSpec(
        num_scalar_prefetch=2,
        grid=(rows // tile,),
        in_specs=[pl.BlockSpec((tile, d), lambda i, te, nt: (i, 0)),
                  pl.BlockSpec((tile, lanes), lambda i, te, nt: (i, 0)),
                  pl.BlockSpec((1, d, fe2), lambda i, te, nt: (te[i], 0, 0)),
                  pl.BlockSpec((1, fe, d), lambda i, te, nt: (te[i], 0, 0))],
        out_specs=pl.BlockSpec((tile, d), lambda i, te, nt: (i, 0)),
    )
    return pl.pallas_call(
        _experts_kernel,
        grid_spec=grid_spec,
        out_shape=jax.ShapeDtypeStruct((rows, d), BF16),
        compiler_params=pltpu.CompilerParams(
            dimension_semantics=("arbitrary",), vmem_limit_bytes=_vmem_limit(blk)),
        name="moe_experts",
    )(tile_expert, n_tiles, xs, gs, w_gu, w_d)


def _combine_ln_kernel(tab_ref, h_ref, meta_ref, ys_ref, g_ref, b_ref, o_ref, grouped_ref, sems, *, alpha, n_experts):
    i = pl.program_id(0)
    buf = i % 2

    def copies(step, b, act):
        _for_each_group_slab(tab_ref, step, n_experts,
                             lambda row, src: act(_group_copy(ys_ref, src, grouped_ref.at[b], row, sems.at[b])))

    @pl.when(i == 0)
    def _():
        grouped_ref[...] = jnp.zeros_like(grouped_ref)
        copies(0, 0, lambda c: c.start())

    @pl.when(i + 1 < pl.num_programs(0))
    def _():
        copies(i + 1, 1 - buf, lambda c: c.start())

    copies(i, buf, lambda c: c.wait())

    tm = h_ref.shape[0]
    rows = grouped_ref.shape[1]
    meta = meta_ref[...]
    slot = lax.broadcasted_iota(jnp.int32, (tm, rows), 1)
    grouped = grouped_ref[buf]
    d1 = meta[:, META_D1:META_D1 + 1].astype(jnp.int32)
    d2 = meta[:, META_D2:META_D2 + 1].astype(jnp.int32)
    pick = jnp.where((slot == d1) | (slot == d2), 1.0, 0.0).astype(BF16)
    moe = jnp.dot(pick, grouped, preferred_element_type=F32)
    o_ref[...] = _layer_norm(alpha * h_ref[...] + moe, g_ref[...], b_ref[...])


def _moe_combine_ln(h2, tab, meta, ys, g, bvec, alpha, n_experts, tm):
    m, d = h2.shape
    lanes = meta.shape[1]
    grouped = _grouped_rows(tm)
    blk = 2 * tm * d * 4 * 2 + 2 * tm * lanes * 4 + 2 * grouped * d * 2 + 2 * tm * grouped * 2 + 3 * tm * d * 4
    return pl.pallas_call(
        functools.partial(_combine_ln_kernel, alpha=alpha, n_experts=n_experts),
        grid=(m // tm,),
        in_specs=[pl.BlockSpec(memory_space=pltpu.SMEM),
                  pl.BlockSpec((tm, d), lambda i: (i, 0)),
                  pl.BlockSpec((tm, lanes), lambda i: (i, 0)),
                  pl.BlockSpec(memory_space=pl.ANY),
                  _const_spec((1, d)), _const_spec((1, d))],
        out_specs=pl.BlockSpec((tm, d), lambda i: (i, 0)),
        out_shape=jax.ShapeDtypeStruct((m, d), F32),
        scratch_shapes=[pltpu.VMEM((2, grouped, d), BF16), pltpu.SemaphoreType.DMA((2,))],
        compiler_params=pltpu.CompilerParams(
            dimension_semantics=("arbitrary",), vmem_limit_bytes=_vmem_limit(blk)),
        name="moe_combine_ln",
    )(tab, h2, meta, ys, g.reshape(1, d), bvec.reshape(1, d))


def _moe_ln(h2, w_router, w_gu, w_d, g, bvec, alpha, tm, tile):
    m, d = h2.shape
    n_experts = w_router.shape[1]
    n_tok_tiles = m // tm
    meta, post, tab = _moe_route(h2, w_router, tm)

    tab = tab[:, 0, :].astype(jnp.int32)
    sizes = tab[:, TAB_SIZE:TAB_SIZE + n_experts]
    region = jnp.sum(sizes, axis=0)
    n_max = -(-(TOP_K * m + n_tok_tiles * n_experts * (GROUP_ALIGN - 1)) // tile) + n_experts
    rows = n_max * tile
    tiles = (region + (tile - 1)) // tile
    ends = jnp.cumsum(tiles)
    n_tiles = ends[-1]
    start = (ends - tiles) * tile
    tab = tab.at[:, TAB_DST:TAB_DST + n_experts].add(start[None, :])
    fill = jnp.stack([jnp.append(start + region, n_tiles * tile),
                      jnp.append(start + tiles * tile, rows)]).astype(jnp.int32)
    idx = jnp.minimum(jnp.arange(n_max, dtype=jnp.int32), n_tiles - 1)
    tile_expert = jnp.sum(idx[:, None] >= ends[None, :], axis=1).astype(jnp.int32)

    xs, gs = _moe_dispatch(h2, post, meta, tab, fill, rows, n_experts, tm)
    ys = _moe_experts(xs, gs, w_gu, w_d, tile_expert, n_tiles.reshape(1).astype(jnp.int32), tile)
    return _moe_combine_ln(h2, tab, meta, ys, g, bvec, alpha, n_experts, tm)


def kernel(x, even_w_in, even_pool_w, even_pool_scale, even_w_out, even_ln1_g, even_ln1_b, even_ffn_w_gu, even_ffn_w_down, even_ln2_g, even_ln2_b, odd_w_in, odd_sgu_ln_g, odd_sgu_ln_b, odd_sgu_w, odd_sgu_b, odd_w_out, odd_ln1_g, odd_ln1_b, odd_router, odd_moe_w_gu, odd_moe_w_down, odd_ln2_g, odd_ln2_b):
    b, s, d = x.shape
    depth = even_w_in.shape[0] + odd_w_in.shape[0]
    alpha = (2.0 * depth) ** 0.25
    pool_width = even_pool_scale.shape[1]
    sb_width = (even_w_in.shape[2] - pool_width) // 3
    m = b * s
    tm = min(ROW_TILE, m)

    h = x.reshape(m, d)
    for layer in range(depth):
        i = layer // 2
        if layer % 2 == 0:
            p, qkv = _in_proj_even(h, even_w_in[i].astype(BF16), pool_width, tm)
            pooled = _pool_mixer(p.reshape(b, s, pool_width), even_pool_w[i].astype(BF16),
                                 even_pool_scale[i], min(POOL_TILE, s))
            attn = _sb_attention(qkv.reshape(b, s, 3 * sb_width), sb_width)
            h = _out_proj_ln(pooled.reshape(m, pool_width), attn.reshape(m, sb_width), h,
                             even_w_out[i].astype(BF16), even_ln1_g[i], even_ln1_b[i], alpha, tm)
            h = _ffn_ln(h, even_ffn_w_gu[i].astype(BF16), even_ffn_w_down[i].astype(BF16),
                        even_ln2_g[i], even_ln2_b[i], alpha, tm, FFN_CHUNK)
        else:
            u, v = _in_proj_odd(h, odd_w_in[i].astype(BF16), odd_sgu_ln_g[i], odd_sgu_ln_b[i], tm)
            h = _sgu_out_ln(u, v, odd_sgu_w[i], odd_sgu_b[i], h, odd_w_out[i].astype(BF16),
                            odd_ln1_g[i], odd_ln1_b[i], alpha, tm)
            h = _moe_ln(h, odd_router[i], odd_moe_w_gu[i].astype(BF16), odd_moe_w_down[i].astype(BF16),
                        odd_ln2_g[i], odd_ln2_b[i], alpha, tm, min(EXPERT_TILE, m))
    return h.reshape(b, s, d)
```
